```python
import jax
import jax.numpy as jnp
from jax import lax
import numpy as np


D_MODEL = 1024
BATCH = 4
SEQ = 8192
DEPTH = 4

GRID_W = 64
CTX_LEN = 256
HEAD_DIM = 64
D_MIX = D_MODEL
CONV_HEADS = 4
CONV_W = CONV_HEADS * HEAD_DIM
CONV_K = 3
SG_HEADS = 4
SG_W = SG_HEADS * HEAD_DIM
SG_CHUNK = 128
NA_HEADS = 4
NA_W = NA_HEADS * HEAD_DIM
NA_KH_MAX = 8
NA_KW = 16
MLA_HEADS = 4
MLA_NOPE = 64
MLA_ROPE = 32
MLA_V = 64
MLA_Q_RANK = 256
MLA_KV_RANK = 128
MLA_W = MLA_HEADS * MLA_V
MLA_SCALE = (MLA_NOPE + MLA_ROPE) ** -0.5
ATTN_QBLOCK = 128
ROPE_BASE = 10000.0
IN_CONV = 3 * CONV_W
IN_SG = 2 * SG_W
IN_NA = 3 * NA_W
IN_MLA = MLA_Q_RANK + MLA_KV_RANK + MLA_ROPE
SG0 = IN_CONV
NA0 = SG0 + IN_SG
MLA0 = NA0 + IN_NA
IN_COLS = MLA0 + IN_MLA
N_GROUPS = 4
EXPERTS_PER_GROUP = 8
N_EXPERTS = N_GROUPS * EXPERTS_PER_GROUP
TOP_K_INNER = 2
D_EXPERT = 512
MOE_BLOCK = 128
N_MOD = 6
EPS = 1e-6

kernel_name = 'hybrid_flow_backbone'


def rms_norm(x, g):
    xf = x.astype(jnp.float32)
    y = xf * lax.rsqrt(jnp.mean(xf * xf, axis=-1, keepdims=True) + EPS)
    return (y * g.astype(jnp.float32)).astype(x.dtype)


def layer_norm_plain(x):
    xf = x.astype(jnp.float32)
    mu = jnp.mean(xf, axis=-1, keepdims=True)
    var = jnp.mean(jnp.square(xf - mu), axis=-1, keepdims=True)
    return ((xf - mu) * lax.rsqrt(var + EPS)).astype(x.dtype)


def modulate(h, shift, scale):
    return h * (1.0 + scale) + shift


def split_heads(t, n_heads):
    return t.reshape(t.shape[:-1] + (n_heads, t.shape[-1] // n_heads))


def axial_rope(n_tok):
    t = jnp.arange(n_tok)
    row = (t // GRID_W).astype(jnp.float32)
    col = (t % GRID_W).astype(jnp.float32)
    n_freq = MLA_ROPE // 4
    inv_freq = ROPE_BASE ** (-jnp.arange(n_freq, dtype=jnp.float32) / n_freq)
    ang_r = row[:, None] * inv_freq
    ang_c = col[:, None] * inv_freq
    ang = jnp.concatenate([ang_r, ang_r, ang_c, ang_c], axis=-1)
    return jnp.cos(ang), jnp.sin(ang)


def apply_rope(x, cos, sin):
    x0, x1, x2, x3 = jnp.split(x, 4, axis=-1)
    rot = jnp.concatenate([-x1, x0, -x3, x2], axis=-1)
    return (x.astype(jnp.float32) * cos + rot.astype(jnp.float32) * sin).astype(x.dtype)


def short_conv_mixer(z, conv_w):
    b_gate, c_gate, hx = jnp.split(z, 3, axis=-1)
    u = c_gate * hx
    y = lax.conv_general_dilated(
        u, conv_w[:, None, :].astype(u.dtype), window_strides=(1,),
        padding=((CONV_K // 2, CONV_K // 2),), dimension_numbers=('NWC', 'WIO', 'NWC'),
        feature_group_count=CONV_W)
    return b_gate * y


def chunk_gating_mixer(z, sg_w, sg_b):
    bsz, n, _ = z.shape
    u, v = jnp.split(jax.nn.gelu(z), 2, axis=-1)
    v = layer_norm_plain(v).reshape(bsz, n // SG_CHUNK, SG_CHUNK, SG_HEADS, HEAD_DIM)
    v = jnp.einsum('hpq,bcqhd->bcphd', sg_w, v) + sg_b.T[None, None, :, :, None]
    return u * v.reshape(bsz, n, SG_W)


def neighborhood_attention(q, k, v, k_ctx, v_ctx, rpb):
    bsz, n, n_h, dh = q.shape
    rows = n // GRID_W
    kh = min(NA_KH_MAX, rows)
    kw = NA_KW
    scale = dh ** -0.5
    r = jnp.arange(rows)
    band = jnp.clip(r - kh // 2, 0, rows - kh)[:, None] + jnp.arange(kh)[None, :]
    col = jnp.arange(GRID_W)
    c0 = jnp.clip(col - kw // 2, 0, GRID_W - kw)
    in_win = (col[None, :] >= c0[:, None]) & (col[None, :] < c0[:, None] + kw)
    dr = band - r[:, None] + (NA_KH_MAX - 1)
    dc = jnp.clip(col[None, :] - col[:, None] + (kw - 1), 0, 2 * kw - 2)
    bias = rpb.astype(jnp.float32)[:, dr[:, None, :, None], dc[None, :, None, :]]
    bias = jnp.where(in_win[None, None, :, None, :], bias, -jnp.inf)
    qg = q.reshape(bsz, rows, GRID_W, n_h, dh)
    k_band = k.reshape(bsz, rows, GRID_W, n_h, dh)[:, band]
    v_band = v.reshape(bsz, rows, GRID_W, n_h, dh)[:, band]
    s_loc = jnp.einsum('brqhd,brjkhd->bhrqjk', qg, k_band).astype(jnp.float32) * scale + bias[None]
    s_ctx = jnp.einsum('brqhd,bchd->bhrqc', qg, k_ctx).astype(jnp.float32) * scale
    n_loc = kh * GRID_W
    s = jnp.concatenate([s_loc.reshape(bsz, n_h, rows, GRID_W, n_loc), s_ctx], axis=-1)
    p = jax.nn.softmax(s, axis=-1).astype(v.dtype)
    p_loc = p[..., :n_loc].reshape(bsz, n_h, rows, GRID_W, kh, GRID_W)
    o = (jnp.einsum('bhrqjk,brjkhd->brqhd', p_loc, v_band)
         + jnp.einsum('bhrqc,bchd->brqhd', p[..., n_loc:], v_ctx))
    return o.reshape(bsz, n, n_h * dh)


def dense_attention(q, k, v):
    s = jnp.einsum('bqhd,bkhd->bhqk', q, k).astype(jnp.float32) * (q.shape[-1] ** -0.5)
    p = jax.nn.softmax(s, axis=-1).astype(v.dtype)
    return jnp.einsum('bhqk,bkhd->bqhd', p, v)


def mla_q(cq, q_norm_g, w_uq):
    q = split_heads(rms_norm(cq, q_norm_g) @ w_uq, MLA_HEADS)
    return q[..., :MLA_NOPE], q[..., MLA_NOPE:]


def mla_kv(ckv_kr, kv_norm_g, w_ukv):
    ckv, k_rope = ckv_kr[..., :MLA_KV_RANK], ckv_kr[..., MLA_KV_RANK:]
    kv = split_heads(rms_norm(ckv, kv_norm_g) @ w_ukv, MLA_HEADS)
    return kv[..., :MLA_NOPE], k_rope, kv[..., MLA_NOPE:]


def mla_attend(q_nope, q_rope, k_nope, k_rope, v):
    s = (jnp.einsum('bqhd,bkhd->bhqk', q_nope, k_nope)
         + jnp.einsum('bqhr,bkr->bhqk', q_rope, k_rope)).astype(jnp.float32) * MLA_SCALE
    p = jax.nn.softmax(s, axis=-1).astype(v.dtype)
    return jnp.einsum('bhqk,bkhd->bqhd', p, v)


def mla_blocked(q_nope, q_rope, k_nope, k_rope, v):
    bsz, n = q_nope.shape[:2]
    nb = n // ATTN_QBLOCK

    def to_blocks(t):
        return jnp.moveaxis(t.reshape((bsz, nb, ATTN_QBLOCK) + t.shape[2:]), 1, 0)

    o = lax.map(lambda qs: mla_attend(qs[0], qs[1], k_nope, k_rope, v),
                (to_blocks(q_nope), to_blocks(q_rope)))
    return jnp.moveaxis(o, 0, 1).reshape(bsz, n, MLA_W)


def merge_groups(ys, out_norm_g, w_out):
    normed = []
    off = 0
    for y in ys:
        w = y.shape[-1]
        normed.append(rms_norm(y, out_norm_g[off:off + w]))
        off += w
    return jnp.concatenate(normed, axis=-1) @ w_out


def hier_moe(h, w_grp, b_grp, w_exp, b_exp, w_gate, w_up, w_down):
    n_tok, d = h.shape
    hf = h.astype(jnp.float32)
    grp_logits = hf @ w_grp.astype(jnp.float32) + b_grp.astype(jnp.float32)
    grp_prob = jax.nn.softmax(grp_logits, axis=-1)
    _, g_sel = lax.top_k(grp_logits, 1)
    p_grp = jnp.take_along_axis(grp_prob, g_sel, axis=-1)
    exp_logits = (hf @ w_exp.astype(jnp.float32) + b_exp.astype(jnp.float32)).reshape(
        n_tok, N_GROUPS, EXPERTS_PER_GROUP)
    idx = jnp.broadcast_to(g_sel[:, :, None], (n_tok, 1, EXPERTS_PER_GROUP))
    in_grp = jnp.take_along_axis(exp_logits, idx, axis=1)[:, 0]
    top_v, top_i = lax.top_k(in_grp, TOP_K_INNER)
    gates = p_grp * jax.nn.softmax(top_v, axis=-1)
    expert_id = (g_sel * EXPERTS_PER_GROUP + top_i).reshape(-1)
    n_asg = n_tok * TOP_K_INNER
    order = jnp.argsort(expert_id)
    sorted_e = expert_id[order]
    counts = jnp.bincount(expert_id, length=N_EXPERTS)
    start = jnp.cumsum(counts) - counts
    padded = (counts + MOE_BLOCK - 1) // MOE_BLOCK * MOE_BLOCK
    padded_end = jnp.cumsum(padded)
    padded_start = padded_end - padded
    dest_sorted = padded_start[sorted_e] + (jnp.arange(n_asg) - start[sorted_e])
    n_blk = -(-n_asg // MOE_BLOCK) + N_EXPERTS
    cap = n_blk * MOE_BLOCK
    row_tok = jnp.full((cap,), n_tok, jnp.int32).at[dest_sorted].set(
        (order // TOP_K_INNER).astype(jnp.int32))
    h_pad = jnp.concatenate([h, jnp.zeros((1, d), h.dtype)], axis=0)
    x_buf = h_pad[row_tok].reshape(n_blk, MOE_BLOCK, d)
    blk_expert = jnp.clip(jnp.searchsorted(padded_end, jnp.arange(n_blk) * MOE_BLOCK, side='right'),
                          0, N_EXPERTS - 1)

    def expert_block(args):
        xb, e = args
        return (jax.nn.silu(xb @ w_gate[e]) * (xb @ w_up[e])) @ w_down[e]

    y_buf = lax.map(expert_block, (x_buf, blk_expert)).reshape(cap, d)
    dest = jnp.zeros((n_asg,), dest_sorted.dtype).at[order].set(dest_sorted)
    y = y_buf[dest].reshape(n_tok, TOP_K_INNER, d).astype(jnp.float32) * gates[..., None]
    return jnp.sum(y, axis=1).astype(h.dtype)


def trunk_layer(x, xc, c_act, c_ctx_act, cos, sin, w_ada, b_ada, norm1_g, norm2_g, w_in,
                conv_w, sg_w, sg_b, na_rpb, mla_q_norm_g, mla_w_uq, mla_kv_norm_g, mla_w_ukv,
                out_norm_g, w_out, w_grp, b_grp, w_exp, b_exp, w_gate, w_up, w_down, update_ctx):
    bsz, n_lat, d = x.shape
    n_ctx = xc.shape[1]
    mod = c_act @ w_ada + b_ada
    mod_c = c_ctx_act @ w_ada + b_ada
    sh1, sc1, g1, sh2, sc2, g2 = jnp.split(mod[:, None, :], N_MOD, axis=-1)
    csh1, csc1, cg1, csh2, csc2, cg2 = jnp.split(mod_c, N_MOD, axis=-1)

    h = modulate(rms_norm(x, norm1_g), sh1, sc1)
    hc = modulate(rms_norm(xc, norm1_g), csh1, csc1)

    z = h @ w_in
    za, zb, zn, zd = jnp.split(z, [SG0, NA0, MLA0], axis=-1)
    if update_ctx:
        zc = hc @ w_in
        zca, zcb, zcn, zcd = jnp.split(zc, [SG0, NA0, MLA0], axis=-1)
        zcn_kv, zcd_kv = zcn[..., NA_W:], zcd[..., MLA_Q_RANK:]
    else:
        zcn_kv = hc @ w_in[:, NA0 + NA_W:MLA0]
        zcd_kv = hc @ w_in[:, MLA0 + MLA_Q_RANK:]

    ya = short_conv_mixer(za, conv_w)
    yb = chunk_gating_mixer(zb, sg_w, sg_b)
    q_na, k_na, v_na = [split_heads(t, NA_HEADS) for t in jnp.split(zn, 3, axis=-1)]
    kc_na, vc_na = [split_heads(t, NA_HEADS) for t in jnp.split(zcn_kv, 2, axis=-1)]
    yc = neighborhood_attention(q_na, k_na, v_na, kc_na, vc_na, na_rpb)
    q_nope, q_rope = mla_q(zd[..., :MLA_Q_RANK], mla_q_norm_g, mla_w_uq)
    q_rope = apply_rope(q_rope, cos[:, None, :], sin[:, None, :])
    k_nope, k_rope, v_m = mla_kv(zd[..., MLA_Q_RANK:], mla_kv_norm_g, mla_w_ukv)
    k_rope = apply_rope(k_rope, cos, sin)
    ck_nope, ck_rope, cv_m = mla_kv(zcd_kv, mla_kv_norm_g, mla_w_ukv)
    yd = mla_blocked(q_nope, q_rope,
                     jnp.concatenate([k_nope, ck_nope], axis=1),
                     jnp.concatenate([k_rope, ck_rope], axis=1),
                     jnp.concatenate([v_m, cv_m], axis=1))
    x = x + g1 * merge_groups([ya, yb, yc, yd], out_norm_g, w_out)
    h2 = modulate(rms_norm(x, norm2_g), sh2, sc2)
    if not update_ctx:
        f = hier_moe(h2.reshape(-1, d), w_grp, b_grp, w_exp, b_exp, w_gate, w_up, w_down)
        return x + g2 * f.reshape(bsz, n_lat, d), xc

    yac = short_conv_mixer(zca, conv_w)
    ybc = chunk_gating_mixer(zcb, sg_w, sg_b)
    ycc = dense_attention(split_heads(zcn[..., :NA_W], NA_HEADS), kc_na, vc_na).reshape(bsz, n_ctx, NA_W)
    cq_nope, cq_rope = mla_q(zcd[..., :MLA_Q_RANK], mla_q_norm_g, mla_w_uq)
    ydc = mla_attend(cq_nope, cq_rope, ck_nope, ck_rope, cv_m).reshape(bsz, n_ctx, MLA_W)
    xc = xc + cg1 * merge_groups([yac, ybc, ycc, ydc], out_norm_g, w_out)
    h2c = modulate(rms_norm(xc, norm2_g), csh2, csc2)
    f = hier_moe(jnp.concatenate([h2.reshape(-1, d), h2c.reshape(-1, d)], axis=0),
                 w_grp, b_grp, w_exp, b_exp, w_gate, w_up, w_down)
    x = x + g2 * f[:bsz * n_lat].reshape(bsz, n_lat, d)
    xc = xc + cg2 * f[bsz * n_lat:].reshape(bsz, n_ctx, d)
    return x, xc


def setup_inputs(seed: int = 0) -> dict:
    key = jax.random.key(seed)
    ks = jax.random.split(key, 27)
    L, D = DEPTH, D_MODEL

    def nrm(i, shape, s):
        return jax.random.normal(ks[i], shape, jnp.float32) * s

    def gain(i, shape):
        return 1.0 + nrm(i, shape, 0.1)

    return {
        'x': nrm(0, (BATCH, SEQ, D), 1.0),
        'c': nrm(1, (BATCH, D), 1.0),
        'ctx': nrm(2, (BATCH, CTX_LEN, D), 1.0),
        'c_ctx': nrm(3, (D,), 1.0),
        'w_ada': nrm(4, (L, D, N_MOD * D), 0.5 * D ** -0.5),
        'b_ada': nrm(5, (L, N_MOD * D), 0.02),
        'norm1_g': gain(6, (L, D)),
        'norm2_g': gain(7, (L, D)),
        'w_in': nrm(8, (L, D, IN_COLS), D ** -0.5),
        'conv_w': nrm(9, (L, CONV_K, CONV_W), CONV_K ** -0.5),
        'sg_w': nrm(10, (L, SG_HEADS, SG_CHUNK, SG_CHUNK), SG_CHUNK ** -0.5),
        'sg_b': gain(11, (L, SG_HEADS, SG_CHUNK)),
        'na_rpb': nrm(12, (L, NA_HEADS, 2 * NA_KH_MAX - 1, 2 * NA_KW - 1), 0.2),
        'mla_q_norm_g': gain(13, (L, MLA_Q_RANK)),
        'mla_w_uq': nrm(14, (L, MLA_Q_RANK, MLA_HEADS * (MLA_NOPE + MLA_ROPE)), MLA_Q_RANK ** -0.5),
        'mla_kv_norm_g': gain(15, (L, MLA_KV_RANK)),
        'mla_w_ukv': nrm(16, (L, MLA_KV_RANK, MLA_HEADS * (MLA_NOPE + MLA_V)), MLA_KV_RANK ** -0.5),
        'out_norm_g': gain(17, (L, D_MIX)),
        'w_out': nrm(18, (L, D_MIX, D), D_MIX ** -0.5),
        'w_grp': nrm(19, (L, D, N_GROUPS), D ** -0.5),
        'b_grp': nrm(20, (L, N_GROUPS), 0.01),
        'w_exp': nrm(21, (L, D, N_EXPERTS), D ** -0.5),
        'b_exp': nrm(22, (L, N_EXPERTS), 0.01),
        'w_gate': nrm(23, (L, N_EXPERTS, D, D_EXPERT), D ** -0.5),
        'w_up': nrm(24, (L, N_EXPERTS, D, D_EXPERT), D ** -0.5),
        'w_down': nrm(25, (L, N_EXPERTS, D_EXPERT, D), D_EXPERT ** -0.5),
        'final_norm_g': gain(26, (D,)),
    }


def reference(x, c, ctx, c_ctx, w_ada, b_ada, norm1_g, norm2_g, w_in, conv_w, sg_w, sg_b, na_rpb,
              mla_q_norm_g, mla_w_uq, mla_kv_norm_g, mla_w_ukv, out_norm_g, w_out,
              w_grp, b_grp, w_exp, b_exp, w_gate, w_up, w_down, final_norm_g):
    cos, sin = axial_rope(x.shape[1])
    c_act = jax.nn.silu(c)
    c_ctx_act = jax.nn.silu(c_ctx)
    xc = ctx
    for l in range(DEPTH):
        x, xc = trunk_layer(
            x, xc, c_act, c_ctx_act, cos, sin, w_ada[l], b_ada[l], norm1_g[l], norm2_g[l], w_in[l],
            conv_w[l], sg_w[l], sg_b[l], na_rpb[l], mla_q_norm_g[l], mla_w_uq[l], mla_kv_norm_g[l],
            mla_w_ukv[l], out_norm_g[l], w_out[l], w_grp[l], b_grp[l], w_exp[l], b_exp[l],
            w_gate[l], w_up[l], w_down[l], update_ctx=(l < DEPTH - 1))
    return rms_norm(x, final_norm_g)
```

```python
import functools

import numpy as np
import jax
import jax.numpy as jnp
from jax import lax
from jax.experimental import pallas as pl
from jax.experimental.pallas import tpu as pltpu

F32 = jnp.float32
BF16 = jnp.bfloat16
I32 = jnp.int32

GRID_W = 64
HEAD_DIM = 64
N_HEADS = 4
GROUP_W = N_HEADS * HEAD_DIM
CONV_K = 3
SG_CHUNK = 128
NA_KH = 8
NA_KW = 16
MLA_NOPE = 64
MLA_ROPE = 32
MLA_Q_RANK = 256
MLA_KV_RANK = 128
MLA_SCALE = (MLA_NOPE + MLA_ROPE) ** -0.5
NA_SCALE = HEAD_DIM ** -0.5
ROPE_BASE = 10000.0
N_GROUPS = 4
EXPERTS_PER_GROUP = 8
N_EXPERTS = N_GROUPS * EXPERTS_PER_GROUP
D_EXPERT = 512
N_MOD = 6
EPS = 1e-6
LANES = 128
HEAD_PAD = 128
MOE_BLOCK = 256
NA_QROWS = 8
NA_KROWS = 16
VMEM_LIMIT = 56 * 1024 * 1024

SG0 = 3 * GROUP_W
NA0 = SG0 + 2 * GROUP_W
MLA0 = NA0 + 3 * GROUP_W
KR0 = MLA0 + MLA_Q_RANK + MLA_KV_RANK
W_IN_EXT = KR0 + 2 * HEAD_PAD


def _params(sem):
    return pltpu.CompilerParams(dimension_semantics=sem, vmem_limit_bytes=VMEM_LIMIT)


def _rms(x, g):
    return x * lax.rsqrt(jnp.mean(x * x, axis=-1, keepdims=True) + EPS) * g


def _dot(a, b):
    return jnp.dot(a, b, preferred_element_type=F32)


def _dot_nt(a, b):
    return lax.dot_general(a, b, (((1,), (1,)), ((), ())), preferred_element_type=F32)


def _head_of_lane(width):
    return lax.broadcasted_iota(I32, (1, width), 1) // HEAD_DIM


def _mod_body(c_ref, w_ref, b_ref, o_ref):
    c = c_ref[...]
    act = (c * jax.nn.sigmoid(c)).astype(BF16)
    o_ref[0] = _dot(act, w_ref[0].astype(BF16)) + b_ref[0]


def _modulation(cc, w_ada, b_ada):
    n_layers, d, n_out = w_ada.shape
    tn = 1536
    return pl.pallas_call(
        _mod_body,
        grid=(n_layers, n_out // tn),
        in_specs=[pl.BlockSpec((8, d), lambda l, j: (0, 0)),
                  pl.BlockSpec((1, d, tn), lambda l, j: (l, 0, j)),
                  pl.BlockSpec((1, 1, tn), lambda l, j: (l, 0, j))],
        out_specs=pl.BlockSpec((1, 8, tn), lambda l, j: (l, 0, j)),
        out_shape=jax.ShapeDtypeStruct((n_layers, 8, n_out), F32),
        compiler_params=_params(("arbitrary", "arbitrary")),
        name="modulation",
    )(cc, w_ada, b_ada.reshape(n_layers, 1, n_out))


def _gelu_tanh(x):
    cdf = 0.5 * (1.0 + jnp.tanh(np.float32(np.sqrt(2 / np.pi)) * (x + 0.044715 * (x * x * x))))
    return x * cdf


def _proj_body(x_ref, mod_ref, n1g_ref, win_ref, sgw_ref, sgb_ref, gb_ref, qng_ref, wq_ref,
               kvg_ref, wkv_ref, cq_ref, ck_ref, sn_ref,
               u_ref, bg_ref, yb_ref, qn_ref, kn_ref, vn_ref, qm_ref, km_ref, vm_ref):
    t = x_ref.shape[0]
    m = mod_ref[0]
    h = _rms(x_ref[...], n1g_ref[...]) * (1.0 + m[1:2]) + m[0:1]
    hb = h.astype(BF16)

    za = _dot(hb, win_ref[:, 0:SG0])
    bg_ref[...] = za[:, 0:GROUP_W]
    u_ref[...] = za[:, GROUP_W:2 * GROUP_W] * za[:, 2 * GROUP_W:3 * GROUP_W]

    g = _gelu_tanh(_dot(hb, win_ref[:, SG0:NA0]))
    ub, vb = g[:, 0:GROUP_W], g[:, GROUP_W:2 * GROUP_W]
    mu = jnp.mean(vb, axis=-1, keepdims=True)
    vc = vb - mu
    vn = (vc * lax.rsqrt(jnp.mean(vc * vc, axis=-1, keepdims=True) + EPS)).astype(BF16)
    head = _head_of_lane(GROUP_W)
    mixed = []
    for c in range(t // SG_CHUNK):
        vch = vn[c * SG_CHUNK:(c + 1) * SG_CHUNK]
        acc = sgb_ref[...]
        for hh in range(N_HEADS):
            acc = acc + _dot(sgw_ref[hh], jnp.where(head == hh, vch, jnp.zeros_like(vch)))
        mixed.append(acc)
    yb = ub * jnp.concatenate(mixed, axis=0)
    yb_ref[...] = _rms(yb, gb_ref[...]).astype(BF16)

    zc = _dot(hb, win_ref[:, NA0:MLA0])
    qn_ref[...] = (zc[:, 0:GROUP_W] * NA_SCALE).astype(BF16)
    kn_ref[...] = zc[:, GROUP_W:2 * GROUP_W].astype(BF16)
    vn_ref[...] = zc[:, 2 * GROUP_W:3 * GROUP_W].astype(BF16)

    zd = _dot(hb, win_ref[:, MLA0:W_IN_EXT])
    cq = zd[:, 0:MLA_Q_RANK]
    ckv = zd[:, MLA_Q_RANK:MLA_Q_RANK + MLA_KV_RANK]
    kr = zd[:, MLA_Q_RANK + MLA_KV_RANK:MLA_Q_RANK + MLA_KV_RANK + HEAD_PAD]
    kr_rot = zd[:, MLA_Q_RANK + MLA_KV_RANK + HEAD_PAD:]
    qq = _dot(_rms(cq, qng_ref[...]).astype(BF16), wq_ref[...])
    kk = _dot(_rms(ckv, kvg_ref[...]).astype(BF16), wkv_ref[...])
    cos_q, cos_k, sin = cq_ref[...], ck_ref[...], sn_ref[...]
    kr_roped = kr * cos_k + kr_rot * sin
    wq_half = N_HEADS * HEAD_PAD
    for hh in range(N_HEADS):
        lo = hh * HEAD_PAD
        q_h = qq[:, lo:lo + HEAD_PAD] * cos_q + qq[:, wq_half + lo:wq_half + lo + HEAD_PAD] * sin
        qm_ref[hh] = (q_h * MLA_SCALE).astype(BF16)
        km_ref[hh] = (kk[:, lo:lo + HEAD_PAD] + kr_roped).astype(BF16)
    vm_ref[...] = kk[:, wq_half:wq_half + GROUP_W].astype(BF16)


def _proj(x_all, mod6, lw, tabs, *, t, n_lat, spt, n_batch):
    nt, d = x_all.shape

    def mod_idx(i):
        return (jnp.where(i < n_lat, i // spt, n_batch), 0, 0)

    def tab_idx(i):
        return (jnp.where(i < n_lat, i % spt, spt), 0)

    def full(a):
        return pl.BlockSpec(a.shape, lambda i, n=a.ndim: (0,) * n)

    row = lambda w: pl.BlockSpec((t, w), lambda i: (i, 0))
    heads = pl.BlockSpec((N_HEADS, t, HEAD_PAD), lambda i: (0, i, 0))
    tab = pl.BlockSpec((t, HEAD_PAD), tab_idx)
    weights = [lw["n1g"], lw["w_in"], lw["sg_w"], lw["sg_b"], lw["g_b"], lw["qng"], lw["w_q"],
               lw["kvg"], lw["w_kv"]]
    out_shape = ([jax.ShapeDtypeStruct((nt, GROUP_W), F32)] * 2
                 + [jax.ShapeDtypeStruct((nt, GROUP_W), BF16)] * 4
                 + [jax.ShapeDtypeStruct((N_HEADS, nt, HEAD_PAD), BF16)] * 2
                 + [jax.ShapeDtypeStruct((nt, GROUP_W), BF16)])
    return pl.pallas_call(
        _proj_body,
        grid=(nt // t,),
        in_specs=[row(d), pl.BlockSpec((1, N_MOD, d), mod_idx)] + [full(w) for w in weights]
                 + [tab, tab, tab],
        out_specs=[row(GROUP_W)] * 6 + [heads, heads, row(GROUP_W)],
        out_shape=out_shape,
        compiler_params=_params(("arbitrary",)),
        name="proj",
    )(x_all, mod6, *weights, *tabs)


def _nbr_body(q_ref, k_ref, v_ref, kc_ref, vc_ref, bias_ref, g_ref, o_ref, acc_ref, *, n_rows):
    i = pl.program_id(1)
    hh = pl.program_id(2)
    start_row = jnp.clip(i * NA_QROWS - NA_KH // 2, 0, n_rows - NA_KROWS)
    start = pl.multiple_of(start_row * GRID_W, GRID_W * 4)
    n_keys = NA_KROWS * GRID_W
    kt = k_ref[pl.ds(start, n_keys), :]
    vt = v_ref[pl.ds(start, n_keys), :]
    q = q_ref[...]
    is_head = _head_of_lane(GROUP_W) == hh
    qh = jnp.where(is_head, q, jnp.zeros_like(q))
    s_loc = _dot_nt(qh, kt) + bias_ref[0, 0]
    s_ctx = _dot_nt(qh, kc_ref[...])
    mx = jnp.maximum(jnp.max(s_loc, axis=-1, keepdims=True), jnp.max(s_ctx, axis=-1, keepdims=True))
    p_loc = jnp.exp(s_loc - mx)
    p_ctx = jnp.exp(s_ctx - mx)
    den = jnp.sum(p_loc, axis=-1, keepdims=True) + jnp.sum(p_ctx, axis=-1, keepdims=True)
    o = (_dot(p_loc.astype(BF16), vt) + _dot(p_ctx.astype(BF16), vc_ref[...])) / den

    @pl.when(hh == 0)
    def _():
        acc_ref[...] = jnp.zeros_like(acc_ref)

    acc_ref[...] = jnp.where(is_head, o, acc_ref[...])

    @pl.when(hh == N_HEADS - 1)
    def _():
        o_ref[...] = _rms(acc_ref[...], g_ref[...]).astype(BF16)


def _nbr_attention(qn, kn, vn, bias, g_c, *, n_batch, seq, ctx_len):
    tq = NA_QROWS * GRID_W
    tiles = seq // tq
    n_rows = seq // GRID_W
    ctx0 = n_batch * seq // ctx_len

    def bias_idx(b, i, h):
        return (jnp.where(i == 0, 0, jnp.where(i == tiles - 1, 2, 1)), h, 0, 0)

    return pl.pallas_call(
        functools.partial(_nbr_body, n_rows=n_rows),
        grid=(n_batch, tiles, N_HEADS),
        in_specs=[pl.BlockSpec((tq, GROUP_W), lambda b, i, h: (b * tiles + i, 0)),
                  pl.BlockSpec((seq, GROUP_W), lambda b, i, h: (b, 0)),
                  pl.BlockSpec((seq, GROUP_W), lambda b, i, h: (b, 0)),
                  pl.BlockSpec((ctx_len, GROUP_W), lambda b, i, h: (ctx0 + b, 0)),
                  pl.BlockSpec((ctx_len, GROUP_W), lambda b, i, h: (ctx0 + b, 0)),
                  pl.BlockSpec((1, 1, tq, NA_KROWS * GRID_W), bias_idx),
                  pl.BlockSpec((1, GROUP_W), lambda b, i, h: (0, 0))],
        out_specs=pl.BlockSpec((tq, GROUP_W), lambda b, i, h: (b * tiles + i, 0)),
        out_shape=jax.ShapeDtypeStruct((n_batch * seq, GROUP_W), BF16),
        scratch_shapes=[pltpu.VMEM((tq, GROUP_W), F32)],
        compiler_params=_params(("arbitrary", "arbitrary", "arbitrary")),
        name="nbr_attention",
    )(qn, kn, vn, kn, vn, bias, g_c)


def _nbr_bias_tables(rpb, n_rows):
    assert n_rows >= NA_KROWS
    col = np.arange(GRID_W)
    c0 = np.clip(col - NA_KW // 2, 0, GRID_W - NA_KW)
    in_win = (col[None, :] >= c0[:, None]) & (col[None, :] < c0[:, None] + NA_KW)
    dc = np.clip(col[None, :] - col[:, None] + (NA_KW - 1), 0, 2 * NA_KW - 2)
    tables = []
    for r0, st in ((0, 0), (NA_QROWS, NA_QROWS - NA_KH // 2), (n_rows - NA_QROWS, n_rows - NA_KROWS)):
        r = r0 + np.arange(NA_QROWS)
        kr = st + np.arange(NA_KROWS)
        lo = np.clip(r - NA_KH // 2, 0, n_rows - NA_KH)
        row_ok = (kr[None, :] >= lo[:, None]) & (kr[None, :] < lo[:, None] + NA_KH)
        dr = np.clip(kr[None, :] - r[:, None] + (NA_KH - 1), 0, 2 * NA_KH - 2)
        ok = row_ok[:, None, :, None] & in_win[None, :, None, :]
        dr_b = np.broadcast_to(dr[:, None, :, None], ok.shape)
        dc_b = np.broadcast_to(dc[None, :, None, :], ok.shape)
        vals = rpb[:, dr_b, dc_b]
        vals = jnp.where(ok[None], vals, -jnp.inf)
        tables.append(vals.reshape(rpb.shape[0], NA_QROWS * GRID_W, NA_KROWS * GRID_W))
    return jnp.stack(tables)


def _mla_body(q_ref, k_ref, v_ref, kc_ref, vc_ref, g_ref, o_ref, m_ref, l_ref, acc_ref, out_ref,
              *, tk, n_chunks):
    hh = pl.program_id(2)
    q = q_ref[0]
    m_ref[...] = jnp.full_like(m_ref, -jnp.inf)
    l_ref[...] = jnp.zeros_like(l_ref)
    acc_ref[...] = jnp.zeros_like(acc_ref)

    def online_step(kt, vt):
        s = _dot_nt(q, kt)
        m_prev = m_ref[...]
        m_new = jnp.maximum(m_prev, jnp.max(s, axis=-1, keepdims=True))
        alpha = jnp.exp(m_prev - m_new)
        p = jnp.exp(s - m_new)
        l_ref[...] = alpha * l_ref[...] + jnp.sum(p, axis=-1, keepdims=True)
        acc_ref[...] = alpha * acc_ref[...] + _dot(p.astype(BF16), vt)
        m_ref[...] = m_new

    def chunk(j, carry):
        off = pl.multiple_of(j * tk, tk)
        online_step(k_ref[0, pl.ds(off, tk), :], v_ref[pl.ds(off, tk), :])
        return carry

    lax.fori_loop(0, n_chunks, chunk, 0)
    online_step(kc_ref[0], vc_ref[...])
    o = acc_ref[...] / l_ref[...]

    @pl.when(hh == 0)
    def _():
        out_ref[...] = jnp.zeros_like(out_ref)

    out_ref[...] = jnp.where(_head_of_lane(GROUP_W) == hh, o, out_ref[...])

    @pl.when(hh == N_HEADS - 1)
    def _():
        o_ref[...] = _rms(out_ref[...], g_ref[...]).astype(BF16)


def _mla_attention(qm, km, vm, g_d, *, n_batch, seq, ctx_len):
    tq = 512
    tk = 512
    tiles = seq // tq
    ctx0 = n_batch * seq // ctx_len
    return pl.pallas_call(
        functools.partial(_mla_body, tk=tk, n_chunks=seq // tk),
        grid=(n_batch, tiles, N_HEADS),
        in_specs=[pl.BlockSpec((1, tq, HEAD_PAD), lambda b, i, h: (h, b * tiles + i, 0)),
                  pl.BlockSpec((1, seq, HEAD_PAD), lambda b, i, h: (h, b, 0)),
                  pl.BlockSpec((seq, GROUP_W), lambda b, i, h: (b, 0)),
                  pl.BlockSpec((1, ctx_len, HEAD_PAD), lambda b, i, h: (h, ctx0 + b, 0)),
                  pl.BlockSpec((ctx_len, GROUP_W), lambda b, i, h: (ctx0 + b, 0)),
                  pl.BlockSpec((1, GROUP_W), lambda b, i, h: (0, 0))],
        out_specs=pl.BlockSpec((tq, GROUP_W), lambda b, i, h: (b * tiles + i, 0)),
        out_shape=jax.ShapeDtypeStruct((n_batch * seq, GROUP_W), BF16),
        scratch_shapes=[pltpu.VMEM((tq, 1), F32), pltpu.VMEM((tq, 1), F32),
                        pltpu.VMEM((tq, GROUP_W), F32), pltpu.VMEM((tq, GROUP_W), F32)],
        compiler_params=_params(("arbitrary", "arbitrary", "arbitrary")),
        name="mla_attention",
    )(qm, km, vm, km, vm, g_d)


def _softmax_rows(s):
    p = jnp.exp(s - jnp.max(s, axis=-1, keepdims=True))
    return p, jnp.sum(p, axis=-1, keepdims=True)


def _ctx_body(qn_ref, kn_ref, vn_ref, qm_ref, km_ref, vm_ref, gc_ref, gd_ref, yc_ref, yd_ref):
    head = _head_of_lane(GROUP_W)
    q = qn_ref[...]
    yc = jnp.zeros(q.shape, F32)
    yd = jnp.zeros(q.shape, F32)
    for hh in range(N_HEADS):
        is_head = head == hh
        p, den = _softmax_rows(_dot_nt(jnp.where(is_head, q, jnp.zeros_like(q)), kn_ref[...]))
        yc = jnp.where(is_head, _dot(p.astype(BF16), vn_ref[...]) / den, yc)
        p, den = _softmax_rows(_dot_nt(qm_ref[hh], km_ref[hh]))
        yd = jnp.where(is_head, _dot(p.astype(BF16), vm_ref[...]) / den, yd)
    yc_ref[...] = _rms(yc, gc_ref[...]).astype(BF16)
    yd_ref[...] = _rms(yd, gd_ref[...]).astype(BF16)


def _ctx_attention(qn, kn, vn, qm, km, vm, g_c, g_d, *, n_batch, seq, ctx_len):
    ctx0 = n_batch * seq // ctx_len
    row = pl.BlockSpec((ctx_len, GROUP_W), lambda b: (ctx0 + b, 0))
    heads = pl.BlockSpec((N_HEADS, ctx_len, HEAD_PAD), lambda b: (0, ctx0 + b, 0))
    gain = pl.BlockSpec((1, GROUP_W), lambda b: (0, 0))
    out = pl.BlockSpec((ctx_len, GROUP_W), lambda b: (b, 0))
    out_shape = jax.ShapeDtypeStruct((n_batch * ctx_len, GROUP_W), BF16)
    return pl.pallas_call(
        _ctx_body,
        grid=(n_batch,),
        in_specs=[row, row, row, heads, heads, row, gain, gain],
        out_specs=[out, out],
        out_shape=[out_shape, out_shape],
        compiler_params=_params(("arbitrary",)),
        name="ctx_attention",
    )(qn, kn, vn, qm, km, vm, g_c, g_d)


def _merge_body(x_ref, u_ref, up_ref, un_ref, bg_ref, yb_ref, yc_ref, yd_ref, ycc_ref, ydc_ref,
                cw_ref, ga_ref, wout_ref, mod_ref, n2g_ref, wrh_ref, wrl_ref, br_ref,
                xm_ref, h2_ref, ri_ref, rf_ref, cnt_ref, run_ref, *, t, n_lat, seq, ctx_len):
    i = pl.program_id(0)
    is_latent = i < n_lat
    yc = jnp.where(is_latent, yc_ref[...], ycc_ref[...])
    yd = jnp.where(is_latent, yd_ref[...], ydc_ref[...])

    seq_len = jnp.where(i < n_lat, seq, ctx_len)
    row = lax.broadcasted_iota(I32, (t, 1), 0)
    pos = lax.rem(i * t + row, seq_len)
    u = u_ref[...]
    u_prev = jnp.where(row == 0, up_ref[7:8, :], pltpu.roll(u, 1, axis=0))
    u_next = jnp.where(row == t - 1, un_ref[0:1, :], pltpu.roll(u, t - 1, axis=0))
    u_prev = jnp.where(pos == 0, 0.0, u_prev)
    u_next = jnp.where(pos == seq_len - 1, 0.0, u_next)
    cw = cw_ref[...]
    ya = bg_ref[...] * (u_prev * cw[0:1] + u * cw[1:2] + u_next * cw[2:3])
    ya = _rms(ya, ga_ref[...]).astype(BF16)

    merged = (_dot(ya, wout_ref[0:GROUP_W, :])
              + _dot(yb_ref[...], wout_ref[GROUP_W:2 * GROUP_W, :])
              + _dot(yc, wout_ref[2 * GROUP_W:3 * GROUP_W, :])
              + _dot(yd, wout_ref[3 * GROUP_W:4 * GROUP_W, :]))
    m = mod_ref[0]
    x_mid = x_ref[...] + m[2:3] * merged
    xm_ref[...] = x_mid
    h2 = _rms(x_mid, n2g_ref[...]) * (1.0 + m[4:5]) + m[3:4]
    h2_ref[...] = h2

    h_hi = h2.astype(BF16)
    h_lo = (h2 - h_hi.astype(F32)).astype(BF16)
    logits = (_dot(h_hi, wrh_ref[...]) + _dot(h_hi, wrl_ref[...]) + _dot(h_lo, wrh_ref[...])
              + br_ref[...])
    lane = lax.broadcasted_iota(I32, (t, LANES), 1)
    neg = jnp.float32(-jnp.inf)
    lg = jnp.where(lane < N_GROUPS, logits, neg)
    g_max = jnp.max(lg, axis=-1, keepdims=True)
    g_sel = jnp.min(jnp.where(lg == g_max, lane, LANES), axis=-1, keepdims=True)
    p_grp = 1.0 / jnp.sum(jnp.where(lane < N_GROUPS, jnp.exp(logits - g_max), 0.0), axis=-1, keepdims=True)
    first = N_GROUPS + EXPERTS_PER_GROUP * g_sel
    le = jnp.where((lane >= first) & (lane < first + EXPERTS_PER_GROUP), logits, neg)
    v1 = jnp.max(le, axis=-1, keepdims=True)
    i1 = jnp.min(jnp.where(le == v1, lane, LANES), axis=-1, keepdims=True)
    le2 = jnp.where(lane == i1, neg, le)
    v2 = jnp.max(le2, axis=-1, keepdims=True)
    i2 = jnp.min(jnp.where(le2 == v2, lane, LANES), axis=-1, keepdims=True)
    tt = jnp.exp(v2 - v1)
    gate0 = p_grp / (1.0 + tt)
    gate1 = p_grp * tt / (1.0 + tt)
    e0 = i1 - N_GROUPS
    e1 = i2 - N_GROUPS

    @pl.when(i == 0)
    def _():
        run_ref[...] = jnp.zeros_like(run_ref)

    sel0 = lane == e0
    sel1 = lane == e1
    onehot = jnp.where(sel0 | sel1, 1.0, 0.0)
    tri = jnp.where(lax.broadcasted_iota(I32, (t, t), 1) < lax.broadcasted_iota(I32, (t, t), 0), 1.0, 0.0)
    before = _dot(tri.astype(BF16), onehot.astype(BF16)) + run_ref[...]
    rank0 = jnp.sum(jnp.where(sel0, before, 0.0), axis=-1, keepdims=True).astype(I32)
    rank1 = jnp.sum(jnp.where(sel1, before, 0.0), axis=-1, keepdims=True).astype(I32)
    run_ref[...] = run_ref[...] + jnp.sum(onehot, axis=0, keepdims=True)
    cnt_ref[...] = jnp.broadcast_to(run_ref[...], cnt_ref.shape).astype(I32)

    ri_ref[...] = jnp.where(lane == 0, e0, jnp.where(lane == 1, e1, jnp.where(lane == 2, rank0, rank1)))
    rf_ref[...] = jnp.where(lane == 0, gate0, gate1)


def _merge(x_all, u, bg, yb, yc, yd, yc_ctx, yd_ctx, mod6, lw, *, t, n_run, n_lat, spt, n_batch,
           seq, ctx_len):
    nt, d = x_all.shape
    hb = t // 8
    n_halo = nt // 8
    latent = pl.BlockSpec((t, GROUP_W), lambda i: (jnp.minimum(i, n_lat - 1), 0))
    context = pl.BlockSpec((t, GROUP_W), lambda i: (jnp.maximum(i - n_lat, 0), 0))

    def mod_idx(i):
        return (jnp.where(i < n_lat, i // spt, n_batch), 0, 0)

    def full(a):
        return pl.BlockSpec(a.shape, lambda i, n=a.ndim: (0,) * n)

    row = lambda w: pl.BlockSpec((t, w), lambda i: (i, 0))
    weights_a = [lw["conv_w"], lw["g_a"], lw["w_out"]]
    weights_b = [lw["n2g"], lw["wr_hi"], lw["wr_lo"], lw["b_r"]]
    rows = n_run * t
    return pl.pallas_call(
        functools.partial(_merge_body, t=t, n_lat=n_lat, seq=seq, ctx_len=ctx_len),
        grid=(n_run,),
        in_specs=[row(d), row(GROUP_W),
                  pl.BlockSpec((8, GROUP_W), lambda i: (jnp.maximum(i * hb - 1, 0), 0)),
                  pl.BlockSpec((8, GROUP_W), lambda i: (jnp.minimum((i + 1) * hb, n_halo - 1), 0)),
                  row(GROUP_W), row(GROUP_W), latent, latent, context, context]
                 + [full(w) for w in weights_a] + [pl.BlockSpec((1, N_MOD, d), mod_idx)]
                 + [full(w) for w in weights_b],
        out_specs=[row(d), row(d), row(LANES), row(LANES), pl.BlockSpec((8, LANES), lambda i: (0, 0))],
        out_shape=[jax.ShapeDtypeStruct((rows, d), F32), jax.ShapeDtypeStruct((rows, d), F32),
                   jax.ShapeDtypeStruct((rows, LANES), I32), jax.ShapeDtypeStruct((rows, LANES), F32),
                   jax.ShapeDtypeStruct((8, LANES), I32)],
        scratch_shapes=[pltpu.VMEM((1, LANES), F32)],
        compiler_params=_params(("arbitrary",)),
        name="merge",
    )(x_all, u, u, u, bg, yb, yc, yd, yc_ctx, yd_ctx, *weights_a, mod6, *weights_b)


def _row_copy(src, src_row, dst, dst_row, sem):
    return pltpu.make_async_copy(src.at[pl.ds(src_row, 1)], dst.at[pl.ds(dst_row, 1)], sem)


def _dispatch_body(dest_ref, h_ref, zero_in, xb_ref, sem, *, t):
    del zero_in

    def issue(j, carry):
        _row_copy(h_ref, j, xb_ref, dest_ref[0, 0, 2 * j], sem).start()
        _row_copy(h_ref, j, xb_ref, dest_ref[0, 0, 2 * j + 1], sem).start()
        return carry

    lax.fori_loop(0, t, issue, 0)

    def drain(j, carry):
        _row_copy(h_ref, 0, xb_ref, 0, sem).wait()
        return carry

    lax.fori_loop(0, 2 * t, drain, 0)


def _dispatch(dest, h2, cap, *, t):
    rows, d = h2.shape
    n_run = rows // t
    return pl.pallas_call(
        functools.partial(_dispatch_body, t=t),
        grid=(n_run,),
        in_specs=[pl.BlockSpec((1, 1, 2 * t), lambda i: (i, 0, 0), memory_space=pltpu.SMEM),
                  pl.BlockSpec((t, d), lambda i: (i, 0)),
                  pl.BlockSpec(memory_space=pl.ANY)],
        out_specs=pl.BlockSpec(memory_space=pl.ANY),
        out_shape=jax.ShapeDtypeStruct((cap, d), F32),
        input_output_aliases={2: 0},
        scratch_shapes=[pltpu.SemaphoreType.DMA],
        compiler_params=_params(("arbitrary",)),
        name="dispatch",
    )(dest.reshape(n_run, 1, 2 * t), h2, jnp.zeros((cap, d), F32))


def _experts_body(blk_ref, used_ref, x_ref, wg_ref, wu_ref, wd_ref, y_ref):
    in_use = pl.program_id(0) < used_ref[0]

    @pl.when(in_use)
    def _():
        xb = x_ref[...].astype(BF16)
        gate = _dot(xb, wg_ref[0].astype(BF16))
        up = _dot(xb, wu_ref[0].astype(BF16))
        act = (gate * jax.nn.sigmoid(gate) * up).astype(BF16)
        y_ref[...] = _dot(act, wd_ref[0].astype(BF16))

    @pl.when(jnp.logical_not(in_use))
    def _():
        y_ref[...] = jnp.zeros_like(y_ref)


def _experts(blk_expert, n_used, x_buf, w_gate, w_up, w_down):
    cap, d = x_buf.shape
    n_blk = cap // MOE_BLOCK

    def x_idx(b, blk, used):
        return (jnp.minimum(b, used[0] - 1), 0)

    def w_idx(b, blk, used):
        return (blk[jnp.minimum(b, used[0] - 1)], 0, 0)

    return pl.pallas_call(
        _experts_body,
        grid_spec=pltpu.PrefetchScalarGridSpec(
            num_scalar_prefetch=2,
            grid=(n_blk,),
            in_specs=[pl.BlockSpec((MOE_BLOCK, d), x_idx),
                      pl.BlockSpec((1, d, D_EXPERT), w_idx),
                      pl.BlockSpec((1, d, D_EXPERT), w_idx),
                      pl.BlockSpec((1, D_EXPERT, d), w_idx)],
            out_specs=pl.BlockSpec((MOE_BLOCK, d), lambda b, blk, used: (b, 0))),
        out_shape=jax.ShapeDtypeStruct((cap, d), F32),
        compiler_params=_params(("arbitrary",)),
        name="experts",
    )(blk_expert, n_used, x_buf, w_gate, w_up, w_down)


def _combine_body(dest_ref, x_ref, rf_ref, mod_ref, fg_ref, y_ref, o_ref, y0_ref, y1_ref, sem,
                  *, t, final):
    def issue(j, carry):
        _row_copy(y_ref, dest_ref[0, 0, 2 * j], y0_ref, j, sem).start()
        _row_copy(y_ref, dest_ref[0, 0, 2 * j + 1], y1_ref, j, sem).start()
        return carry

    lax.fori_loop(0, t, issue, 0)

    def drain(j, carry):
        _row_copy(y_ref, 0, y0_ref, 0, sem).wait()
        return carry

    lax.fori_loop(0, 2 * t, drain, 0)
    rf = rf_ref[...]
    f = y0_ref[...] * rf[:, 0:1] + y1_ref[...] * rf[:, 1:2]
    x_new = x_ref[...] + mod_ref[0][5:6] * f
    o_ref[...] = _rms(x_new, fg_ref[...]) if final else x_new


def _combine(dest, x_mid, rf, mod6, final_g, y_buf, *, t, n_lat, spt, n_batch, final):
    rows, d = x_mid.shape
    n_run = rows // t

    def mod_idx(i):
        return (jnp.where(i < n_lat, i // spt, n_batch), 0, 0)

    return pl.pallas_call(
        functools.partial(_combine_body, t=t, final=final),
        grid=(n_run,),
        in_specs=[pl.BlockSpec((1, 1, 2 * t), lambda i: (i, 0, 0), memory_space=pltpu.SMEM),
                  pl.BlockSpec((t, d), lambda i: (i, 0)),
                  pl.BlockSpec((t, LANES), lambda i: (i, 0)),
                  pl.BlockSpec((1, N_MOD, d), mod_idx),
                  pl.BlockSpec((1, d), lambda i: (0, 0)),
                  pl.BlockSpec(memory_space=pl.ANY)],
        out_specs=pl.BlockSpec((t, d), lambda i: (i, 0)),
        out_shape=jax.ShapeDtypeStruct((rows, d), F32),
        scratch_shapes=[pltpu.VMEM((t, d), F32), pltpu.VMEM((t, d), F32), pltpu.SemaphoreType.DMA],
        compiler_params=_params(("arbitrary",)),
        name="combine",
    )(dest.reshape(n_run, 1, 2 * t), x_mid, rf, mod6, final_g, y_buf)


def _route_plan(ri, counts, cap):
    cnt = counts[0, :N_EXPERTS]
    padded = (cnt + MOE_BLOCK - 1) // MOE_BLOCK * MOE_BLOCK
    padded_end = jnp.cumsum(padded)
    padded_start = padded_end - padded
    dest = jnp.stack([padded_start[ri[:, 0]] + ri[:, 2], padded_start[ri[:, 1]] + ri[:, 3]], axis=1)
    n_blk = cap // MOE_BLOCK
    blk_expert = jnp.clip(jnp.searchsorted(padded_end, jnp.arange(n_blk, dtype=I32) * MOE_BLOCK, side="right"),
                          0, N_EXPERTS - 1).astype(I32)
    n_used = (padded_end[-1:] // MOE_BLOCK).astype(I32)
    return dest.astype(I32), blk_expert, n_used


def _rope_tables(seq, t):
    pos = jnp.arange(seq)
    row = (pos // GRID_W).astype(F32)
    col = (pos % GRID_W).astype(F32)
    n_freq = MLA_ROPE // 4
    inv_freq = ROPE_BASE ** (-jnp.arange(n_freq, dtype=F32) / n_freq)
    ang_r = row[:, None] * inv_freq
    ang_c = col[:, None] * inv_freq
    ang = jnp.concatenate([ang_r, ang_r, ang_c, ang_c], axis=-1)
    cos = jnp.concatenate([jnp.cos(ang), jnp.ones((t, MLA_ROPE), F32)], axis=0)
    sin = jnp.concatenate([jnp.sin(ang), jnp.zeros((t, MLA_ROPE), F32)], axis=0)
    n = seq + t
    pad = jnp.zeros((n, HEAD_PAD - MLA_NOPE - MLA_ROPE), F32)
    cos_q = jnp.concatenate([jnp.ones((n, MLA_NOPE), F32), cos, pad], axis=1)
    cos_k = jnp.concatenate([jnp.zeros((n, MLA_NOPE), F32), cos, pad], axis=1)
    sin_t = jnp.concatenate([jnp.zeros((n, MLA_NOPE), F32), sin, pad], axis=1)
    return cos_q, cos_k, sin_t


_ROT_SRC = np.concatenate([np.arange(8, 16), np.arange(0, 8), np.arange(24, 32), np.arange(16, 24)])
_ROT_SIGN = np.concatenate([-np.ones(8), np.ones(8), -np.ones(8), np.ones(8)]).astype(np.float32)


def _rot_cols(w):
    return w[..., _ROT_SRC] * _ROT_SIGN


def _layer_weights(p, l):
    d = p["w_in"].shape[1]
    w_in = p["w_in"][l]
    kr = w_in[:, KR0:KR0 + MLA_ROPE]
    zl = jnp.zeros((d, MLA_NOPE), F32)
    zr = jnp.zeros((d, HEAD_PAD - MLA_NOPE - MLA_ROPE), F32)
    w_in_ext = jnp.concatenate([w_in[:, :KR0], zl, kr, zr, zl, _rot_cols(kr), zr], axis=1)

    w_uq = p["mla_w_uq"][l].reshape(MLA_Q_RANK, N_HEADS, MLA_NOPE + MLA_ROPE)
    zq = jnp.zeros((MLA_Q_RANK, N_HEADS, HEAD_PAD - MLA_NOPE - MLA_ROPE), F32)
    q_plain = jnp.concatenate([w_uq, zq], axis=-1).reshape(MLA_Q_RANK, N_HEADS * HEAD_PAD)
    q_rot = jnp.concatenate([jnp.zeros((MLA_Q_RANK, N_HEADS, MLA_NOPE), F32),
                             _rot_cols(w_uq[..., MLA_NOPE:]), zq], axis=-1).reshape(MLA_Q_RANK, N_HEADS * HEAD_PAD)
    w_q = jnp.concatenate([q_plain, q_rot], axis=1)

    w_ukv = p["mla_w_ukv"][l].reshape(MLA_KV_RANK, N_HEADS, MLA_NOPE + HEAD_DIM)
    k_cols = jnp.concatenate([w_ukv[..., :MLA_NOPE], jnp.zeros((MLA_KV_RANK, N_HEADS, HEAD_PAD - MLA_NOPE), F32)],
                             axis=-1).reshape(MLA_KV_RANK, N_HEADS * HEAD_PAD)
    v_cols = w_ukv[..., MLA_NOPE:].reshape(MLA_KV_RANK, GROUP_W)
    w_kv = jnp.concatenate([k_cols, v_cols], axis=1)

    w_r = jnp.concatenate([p["w_grp"][l], p["w_exp"][l], jnp.zeros((d, LANES - N_GROUPS - N_EXPERTS), F32)], axis=1)
    wr_hi = w_r.astype(BF16)
    b_r = jnp.concatenate([p["b_grp"][l], p["b_exp"][l], jnp.zeros((LANES - N_GROUPS - N_EXPERTS,), F32)])[None]
    g_out = p["out_norm_g"][l]
    return dict(
        n1g=p["norm1_g"][l][None], n2g=p["norm2_g"][l][None], w_in=w_in_ext.astype(BF16),
        sg_w=p["sg_w"][l].astype(BF16), sg_b=jnp.repeat(p["sg_b"][l].T, HEAD_DIM, axis=1),
        g_a=g_out[None, 0:GROUP_W], g_b=g_out[None, GROUP_W:2 * GROUP_W],
        g_c=g_out[None, 2 * GROUP_W:3 * GROUP_W], g_d=g_out[None, 3 * GROUP_W:4 * GROUP_W],
        qng=p["mla_q_norm_g"][l][None], w_q=w_q.astype(BF16), kvg=p["mla_kv_norm_g"][l][None],
        w_kv=w_kv.astype(BF16), conv_w=p["conv_w"][l], w_out=p["w_out"][l].astype(BF16),
        wr_hi=wr_hi, wr_lo=(w_r - wr_hi.astype(F32)).astype(BF16), b_r=b_r)


def kernel(x, c, ctx, c_ctx, w_ada, b_ada, norm1_g, norm2_g, w_in, conv_w, sg_w, sg_b, na_rpb, mla_q_norm_g, mla_w_uq, mla_kv_norm_g, mla_w_ukv, out_norm_g, w_out, w_grp, b_grp, w_exp, b_exp, w_gate, w_up, w_down, final_norm_g):
    n_batch, seq, d = x.shape
    ctx_len = ctx.shape[1]
    n_layers = w_ada.shape[0]
    t = 512 if (n_batch * ctx_len) % 512 == 0 else 256
    assert seq % 512 == 0 and seq % ctx_len == 0 and t % ctx_len == 0 and n_batch < 8
    p = dict(w_in=w_in, mla_w_uq=mla_w_uq, mla_w_ukv=mla_w_ukv, w_grp=w_grp, w_exp=w_exp, b_grp=b_grp,
             b_exp=b_exp, out_norm_g=out_norm_g, norm1_g=norm1_g, norm2_g=norm2_g, sg_w=sg_w, sg_b=sg_b,
             mla_q_norm_g=mla_q_norm_g, mla_kv_norm_g=mla_kv_norm_g, conv_w=conv_w, w_out=w_out)

    n_lat = n_batch * seq // t
    spt = seq // t
    nt = n_batch * (seq + ctx_len)
    geo = dict(n_batch=n_batch, seq=seq, ctx_len=ctx_len)
    tile = dict(t=t, n_lat=n_lat, spt=spt, n_batch=n_batch)

    x_all = jnp.concatenate([x.reshape(n_batch * seq, d), ctx.reshape(n_batch * ctx_len, d)], axis=0)
    cc = jnp.concatenate([c, c_ctx[None], jnp.zeros((8 - n_batch - 1, d), F32)], axis=0)
    mod_all = _modulation(cc, w_ada, b_ada).reshape(n_layers, 8, N_MOD, d)
    tabs = _rope_tables(seq, t)
    final_g = final_norm_g[None]

    for l in range(n_layers):
        update_ctx = l < n_layers - 1
        lw = _layer_weights(p, l)
        mod6 = mod_all[l]
        u, bg, yb, qn, kn, vn, qm, km, vm = _proj(x_all, mod6, lw, tabs, **tile)
        bias = _nbr_bias_tables(na_rpb[l], seq // GRID_W)
        yc = _nbr_attention(qn, kn, vn, bias, lw["g_c"], **geo)
        yd = _mla_attention(qm, km, vm, lw["g_d"], **geo)
        if update_ctx:
            yc_ctx, yd_ctx = _ctx_attention(qn, kn, vn, qm, km, vm, lw["g_c"], lw["g_d"], **geo)
        else:
            yc_ctx = yd_ctx = jnp.zeros((n_batch * ctx_len, GROUP_W), BF16)
        n_run = nt // t if update_ctx else n_lat
        x_mid, h2, ri, rf, counts = _merge(x_all, u, bg, yb, yc, yd, yc_ctx, yd_ctx, mod6, lw, n_run=n_run,
                                           seq=seq, ctx_len=ctx_len, **tile)
        n_asg = 2 * n_run * t
        cap = (-(-n_asg // MOE_BLOCK) + N_EXPERTS) * MOE_BLOCK
        dest, blk_expert, n_used = _route_plan(ri, counts, cap)
        x_buf = _dispatch(dest, h2, cap, t=t)
        y_buf = _experts(blk_expert, n_used, x_buf, w_gate[l], w_up[l], w_down[l])
        x_all = _combine(dest, x_mid, rf, mod6, final_g, y_buf, final=not update_ctx, **tile)
    return x_all.reshape(n_batch, seq, d)
```

```python
import functools

import numpy as np
import jax
import jax.numpy as jnp
from jax import lax
from jax.experimental import pallas as pl
from jax.experimental.pallas import tpu as pltpu

F32 = jnp.float32
BF16 = jnp.bfloat16
I32 = jnp.int32

GRID_W = 64
HEAD_DIM = 64
N_HEADS = 4
GROUP_W = N_HEADS * HEAD_DIM
CONV_K = 3
SG_CHUNK = 128
NA_KH = 8
NA_KW = 16
MLA_NOPE = 64
MLA_ROPE = 32
MLA_Q_RANK = 256
MLA_KV_RANK = 128
MLA_SCALE = (MLA_NOPE + MLA_ROPE) ** -0.5
NA_SCALE = HEAD_DIM ** -0.5
ROPE_BASE = 10000.0
N_GROUPS = 4
EXPERTS_PER_GROUP = 8
N_EXPERTS = N_GROUPS * EXPERTS_PER_GROUP
D_EXPERT = 512
N_MOD = 6
EPS = 1e-6
LANES = 128
HEAD_PAD = 128
SUM_ROWS = 16
LOG2E = float(np.log2(np.e))
MOE_BLOCK = 256
NA_QROWS = 8
NA_KROWS = 16
VMEM_LIMIT = 56 * 1024 * 1024

SG0 = 3 * GROUP_W
NA0 = SG0 + 2 * GROUP_W
MLA0 = NA0 + 3 * GROUP_W
KR0 = MLA0 + MLA_Q_RANK + MLA_KV_RANK
W_IN_EXT = KR0 + 2 * HEAD_PAD


def _params(sem):
    return pltpu.CompilerParams(dimension_semantics=sem, vmem_limit_bytes=VMEM_LIMIT)


def _rms(x, g):
    return x * lax.rsqrt(jnp.mean(x * x, axis=-1, keepdims=True) + EPS) * g


def _dot(a, b):
    return jnp.dot(a, b, preferred_element_type=F32)


def _dot_nt(a, b):
    return lax.dot_general(a, b, (((1,), (1,)), ((), ())), preferred_element_type=F32)


def _head_of_lane(width):
    return lax.broadcasted_iota(I32, (1, width), 1) // HEAD_DIM


def _mod_body(c_ref, w_ref, b_ref, o_ref):
    c = c_ref[...]
    act = (c * jax.nn.sigmoid(c)).astype(BF16)
    o_ref[0] = _dot(act, w_ref[0].astype(BF16)) + b_ref[0]


def _modulation(cc, w_ada, b_ada):
    n_layers, d, n_out = w_ada.shape
    tn = 1536
    return pl.pallas_call(
        _mod_body,
        grid=(n_layers, n_out // tn),
        in_specs=[pl.BlockSpec((8, d), lambda l, j: (0, 0)),
                  pl.BlockSpec((1, d, tn), lambda l, j: (l, 0, j)),
                  pl.BlockSpec((1, 1, tn), lambda l, j: (l, 0, j))],
        out_specs=pl.BlockSpec((1, 8, tn), lambda l, j: (l, 0, j)),
        out_shape=jax.ShapeDtypeStruct((n_layers, 8, n_out), F32),
        compiler_params=_params(("arbitrary", "arbitrary")),
        name="modulation",
    )(cc, w_ada, b_ada.reshape(n_layers, 1, n_out))


def _gelu_tanh(x):
    cdf = 0.5 * (1.0 + jnp.tanh(np.float32(np.sqrt(2 / np.pi)) * (x + 0.044715 * (x * x * x))))
    return x * cdf


def _proj_body(x_ref, mod_ref, n1g_ref, win_ref, sgw_ref, sgb_ref, gb_ref, qng_ref, wq_ref,
               kvg_ref, wkv_ref, cq_ref, ck_ref, sn_ref,
               u_ref, bg_ref, yb_ref, qn_ref, kn_ref, vn_ref, qm_ref, km_ref, vm_ref):
    t = x_ref.shape[0]
    m = mod_ref[0]
    h = _rms(x_ref[...], n1g_ref[...]) * (1.0 + m[1:2]) + m[0:1]
    hb = h.astype(BF16)

    za = _dot(hb, win_ref[:, 0:SG0])
    bg_ref[...] = za[:, 0:GROUP_W]
    u_ref[...] = za[:, GROUP_W:2 * GROUP_W] * za[:, 2 * GROUP_W:3 * GROUP_W]

    g = _gelu_tanh(_dot(hb, win_ref[:, SG0:NA0]))
    ub, vb = g[:, 0:GROUP_W], g[:, GROUP_W:2 * GROUP_W]
    mu = jnp.mean(vb, axis=-1, keepdims=True)
    vc = vb - mu
    vn = (vc * lax.rsqrt(jnp.mean(vc * vc, axis=-1, keepdims=True) + EPS)).astype(BF16)
    head = _head_of_lane(GROUP_W)
    mixed = []
    for c in range(t // SG_CHUNK):
        vch = vn[c * SG_CHUNK:(c + 1) * SG_CHUNK]
        acc = sgb_ref[...]
        for hh in range(N_HEADS):
            acc = acc + _dot(sgw_ref[hh], jnp.where(head == hh, vch, jnp.zeros_like(vch)))
        mixed.append(acc)
    yb = ub * jnp.concatenate(mixed, axis=0)
    yb_ref[...] = _rms(yb, gb_ref[...]).astype(BF16)

    zc = _dot(hb, win_ref[:, NA0:MLA0])
    qn_ref[...] = (zc[:, 0:GROUP_W] * (NA_SCALE * LOG2E)).astype(BF16)
    kn_ref[...] = zc[:, GROUP_W:2 * GROUP_W].astype(BF16)
    vn_ref[...] = zc[:, 2 * GROUP_W:3 * GROUP_W].T.astype(BF16)

    zd = _dot(hb, win_ref[:, MLA0:W_IN_EXT])
    cq = zd[:, 0:MLA_Q_RANK]
    ckv = zd[:, MLA_Q_RANK:MLA_Q_RANK + MLA_KV_RANK]
    kr = zd[:, MLA_Q_RANK + MLA_KV_RANK:MLA_Q_RANK + MLA_KV_RANK + HEAD_PAD]
    kr_rot = zd[:, MLA_Q_RANK + MLA_KV_RANK + HEAD_PAD:]
    qq = _dot(_rms(cq, qng_ref[...]).astype(BF16), wq_ref[...])
    kk = _dot(_rms(ckv, kvg_ref[...]).astype(BF16), wkv_ref[...])
    cos_q, cos_k, sin = cq_ref[...], ck_ref[...], sn_ref[...]
    kr_roped = kr * cos_k + kr_rot * sin
    wq_half = N_HEADS * HEAD_PAD
    for hh in range(N_HEADS):
        lo = hh * HEAD_PAD
        q_h = qq[:, lo:lo + HEAD_PAD] * cos_q + qq[:, wq_half + lo:wq_half + lo + HEAD_PAD] * sin
        qm_ref[hh] = (q_h * (MLA_SCALE * LOG2E)).astype(BF16)
        km_ref[hh] = (kk[:, lo:lo + HEAD_PAD] + kr_roped).astype(BF16)
    vm_ref[...] = kk[:, wq_half:wq_half + GROUP_W].T.astype(BF16)


def _proj(x_all, mod6, lw, tabs, *, t, n_lat, spt, n_batch):
    nt, d = x_all.shape

    def mod_idx(i):
        return (jnp.where(i < n_lat, i // spt, n_batch), 0, 0)

    def tab_idx(i):
        return (jnp.where(i < n_lat, i % spt, spt), 0)

    def full(a):
        return pl.BlockSpec(a.shape, lambda i, n=a.ndim: (0,) * n)

    row = lambda w: pl.BlockSpec((t, w), lambda i: (i, 0))
    heads = pl.BlockSpec((N_HEADS, t, HEAD_PAD), lambda i: (0, i, 0))
    tab = pl.BlockSpec((t, HEAD_PAD), tab_idx)
    weights = [lw["n1g"], lw["w_in"], lw["sg_w"], lw["sg_b"], lw["g_b"], lw["qng"], lw["w_q"],
               lw["kvg"], lw["w_kv"]]
    cols = pl.BlockSpec((GROUP_W, t), lambda i: (0, i))
    out_shape = ([jax.ShapeDtypeStruct((nt, GROUP_W), F32)] * 2
                 + [jax.ShapeDtypeStruct((nt, GROUP_W), BF16)] * 3
                 + [jax.ShapeDtypeStruct((GROUP_W, nt), BF16)]
                 + [jax.ShapeDtypeStruct((N_HEADS, nt, HEAD_PAD), BF16)] * 2
                 + [jax.ShapeDtypeStruct((GROUP_W, nt), BF16)])
    return pl.pallas_call(
        _proj_body,
        grid=(nt // t,),
        in_specs=[row(d), pl.BlockSpec((1, N_MOD, d), mod_idx)] + [full(w) for w in weights]
                 + [tab, tab, tab],
        out_specs=[row(GROUP_W)] * 5 + [cols, heads, heads, cols],
        out_shape=out_shape,
        compiler_params=_params(("arbitrary",)),
        name="proj",
    )(x_all, mod6, *weights, *tabs)


def _nbr_body(q_ref, k_ref, vt_ref, kc_ref, vtc_ref, bias_ref, g_ref, o_ref, s_ref, *, n_rows):
    i = pl.program_id(1)
    start_row = jnp.clip(i * NA_QROWS - NA_KH // 2, 0, n_rows - NA_KROWS)
    start = pl.multiple_of(start_row * GRID_W, GRID_W * 4)
    n_loc = NA_KROWS * GRID_W
    n_keys = s_ref.shape[1]
    kt = k_ref[pl.ds(start, n_loc), :]
    kc = kc_ref[...]
    q = q_ref[...]
    head = _head_of_lane(GROUP_W)
    ones = jnp.ones((SUM_ROWS, n_keys), BF16)

    def scores(hh):
        qh = jnp.where(head == hh, q, jnp.zeros_like(q))
        s_ref[hh % 2, 0:n_loc, :] = _dot_nt(kt, qh) + bias_ref[0, hh]
        s_ref[hh % 2, n_loc:n_keys, :] = _dot_nt(kc, qh)

    def attend(hh):
        rows = pl.ds(hh * HEAD_DIM, HEAD_DIM)
        s = s_ref[hh % 2]
        p = jnp.exp2(s - jnp.max(s, axis=0, keepdims=True)).astype(BF16)
        values_t = jnp.concatenate([vt_ref[rows, pl.ds(start, n_loc)], vtc_ref[rows, :]], axis=1)
        a = _dot(jnp.concatenate([values_t, ones], axis=0), p)
        return a[0:HEAD_DIM] / a[HEAD_DIM:HEAD_DIM + 1]

    heads = []
    scores(0)
    for hh in range(N_HEADS):
        if hh + 1 < N_HEADS:
            scores(hh + 1)
        heads.append(attend(hh))
    o = jnp.concatenate(heads, axis=0)
    y = o * lax.rsqrt(jnp.mean(o * o, axis=0, keepdims=True) + EPS) * g_ref[...]
    o_ref[...] = y.T.astype(BF16)


def _nbr_attention(qn, kn, vnt, bias, g_c, *, n_batch, seq, ctx_len):
    tq = NA_QROWS * GRID_W
    tiles = seq // tq
    n_rows = seq // GRID_W
    ctx0 = n_batch * seq // ctx_len

    def bias_idx(b, i):
        return (jnp.where(i == 0, 0, jnp.where(i == tiles - 1, 2, 1)), 0, 0, 0)

    return pl.pallas_call(
        functools.partial(_nbr_body, n_rows=n_rows),
        grid=(n_batch, tiles),
        in_specs=[pl.BlockSpec((tq, GROUP_W), lambda b, i: (b * tiles + i, 0)),
                  pl.BlockSpec((seq, GROUP_W), lambda b, i: (b, 0)),
                  pl.BlockSpec((GROUP_W, seq), lambda b, i: (0, b)),
                  pl.BlockSpec((ctx_len, GROUP_W), lambda b, i: (ctx0 + b, 0)),
                  pl.BlockSpec((GROUP_W, ctx_len), lambda b, i: (0, ctx0 + b)),
                  pl.BlockSpec((1, N_HEADS, NA_KROWS * GRID_W, tq), bias_idx),
                  pl.BlockSpec((GROUP_W, 1), lambda b, i: (0, 0))],
        out_specs=pl.BlockSpec((tq, GROUP_W), lambda b, i: (b * tiles + i, 0)),
        out_shape=jax.ShapeDtypeStruct((n_batch * seq, GROUP_W), BF16),
        scratch_shapes=[pltpu.VMEM((2, NA_KROWS * GRID_W + ctx_len, tq), F32)],
        compiler_params=_params(("arbitrary", "arbitrary")),
        name="nbr_attention",
    )(qn, kn, vnt, kn, vnt, bias, g_c.reshape(GROUP_W, 1))


def _nbr_bias_tables(rpb, n_rows):
    assert n_rows >= NA_KROWS
    col = np.arange(GRID_W)
    c0 = np.clip(col - NA_KW // 2, 0, GRID_W - NA_KW)
    in_win = (col[:, None] >= c0[None, :]) & (col[:, None] < c0[None, :] + NA_KW)
    dc = np.clip(col[:, None] - col[None, :] + (NA_KW - 1), 0, 2 * NA_KW - 2)
    blocks = jnp.where(in_win[None, None], rpb[:, :, dc] * LOG2E, -jnp.inf)
    masked = jnp.full((rpb.shape[0], GRID_W, GRID_W), -jnp.inf, F32)
    tables = []
    for r0, st in ((0, 0), (NA_QROWS, NA_QROWS - NA_KH // 2), (n_rows - NA_QROWS, n_rows - NA_KROWS)):
        strips = []
        for j in range(NA_KROWS):
            strip = []
            for a in range(NA_QROWS):
                r = r0 + a
                lo = min(max(r - NA_KH // 2, 0), n_rows - NA_KH)
                strip.append(blocks[:, st + j - r + NA_KH - 1] if lo <= st + j < lo + NA_KH else masked)
            strips.append(jnp.concatenate(strip, axis=-1))
        tables.append(jnp.concatenate(strips, axis=1))
    return jnp.stack(tables)


def _mla_body(q_ref, k_ref, vt_ref, kc_ref, vtc_ref, g_ref, o_ref, m_ref, acc_ref, s_ref,
              *, tk, n_chunks):
    m_ref[...] = jnp.full_like(m_ref, -jnp.inf)
    acc_ref[...] = jnp.zeros_like(acc_ref)

    def absorb(hh, s, values_t):
        m_prev = m_ref[hh]
        m_new = jnp.maximum(m_prev, jnp.max(s, axis=0, keepdims=True))
        p = jnp.exp2(s - m_new).astype(BF16)
        lhs = jnp.concatenate([values_t, jnp.ones((SUM_ROWS, s.shape[0]), BF16)], axis=0)
        acc_ref[hh] = jnp.exp2(m_prev - m_new) * acc_ref[hh] + _dot(lhs, p)
        m_ref[hh] = m_new

    def scores(hh, off):
        return _dot_nt(k_ref[hh, pl.ds(off, tk), :], q_ref[hh])

    s_ref[0] = scores(0, 0)

    def chunk(j, carry):
        off = pl.multiple_of(j * tk, tk)
        off_next = pl.multiple_of(jnp.minimum(j + 1, n_chunks - 1) * tk, tk)
        for hh in range(N_HEADS):
            if hh + 1 < N_HEADS:
                s_ref[(hh + 1) % 2] = scores(hh + 1, off)
            else:
                s_ref[(hh + 1) % 2] = scores(0, off_next)
            absorb(hh, s_ref[hh % 2], vt_ref[pl.ds(hh * HEAD_DIM, HEAD_DIM), pl.ds(off, tk)])
        return carry

    lax.fori_loop(0, n_chunks, chunk, 0, unroll=2)
    for hh in range(N_HEADS):
        absorb(hh, _dot_nt(kc_ref[hh], q_ref[hh]), vtc_ref[pl.ds(hh * HEAD_DIM, HEAD_DIM), :])

    heads = []
    for hh in range(N_HEADS):
        a = acc_ref[hh]
        heads.append(a[0:HEAD_DIM] / a[HEAD_DIM:HEAD_DIM + 1])
    o = jnp.concatenate(heads, axis=0)
    y = o * lax.rsqrt(jnp.mean(o * o, axis=0, keepdims=True) + EPS) * g_ref[...]
    o_ref[...] = y.T.astype(BF16)


def _mla_attention(qm, km, vmt, g_d, *, n_batch, seq, ctx_len):
    tq = 512
    tk = 512
    tiles = seq // tq
    ctx0 = n_batch * seq // ctx_len
    return pl.pallas_call(
        functools.partial(_mla_body, tk=tk, n_chunks=seq // tk),
        grid=(n_batch, tiles),
        in_specs=[pl.BlockSpec((N_HEADS, tq, HEAD_PAD), lambda b, i: (0, b * tiles + i, 0)),
                  pl.BlockSpec((N_HEADS, seq, HEAD_PAD), lambda b, i: (0, b, 0)),
                  pl.BlockSpec((GROUP_W, seq), lambda b, i: (0, b)),
                  pl.BlockSpec((N_HEADS, ctx_len, HEAD_PAD), lambda b, i: (0, ctx0 + b, 0)),
                  pl.BlockSpec((GROUP_W, ctx_len), lambda b, i: (0, ctx0 + b)),
                  pl.BlockSpec((GROUP_W, 1), lambda b, i: (0, 0))],
        out_specs=pl.BlockSpec((tq, GROUP_W), lambda b, i: (b * tiles + i, 0)),
        out_shape=jax.ShapeDtypeStruct((n_batch * seq, GROUP_W), BF16),
        scratch_shapes=[pltpu.VMEM((N_HEADS, 1, tq), F32),
                        pltpu.VMEM((N_HEADS, HEAD_DIM + SUM_ROWS, tq), F32),
                        pltpu.VMEM((2, tk, tq), F32)],
        compiler_params=_params(("arbitrary", "arbitrary")),
        name="mla_attention",
    )(qm, km, vmt, km, vmt, g_d.reshape(GROUP_W, 1))


def _softmax2_rows(s):
    p = jnp.exp2(s - jnp.max(s, axis=-1, keepdims=True))
    return p, jnp.sum(p, axis=-1, keepdims=True)


def _ctx_body(qn_ref, kn_ref, vnt_ref, qm_ref, km_ref, vmt_ref, gc_ref, gd_ref, yc_ref, yd_ref):
    head = _head_of_lane(GROUP_W)
    q = qn_ref[...]
    yc = jnp.zeros(q.shape, F32)
    yd = jnp.zeros(q.shape, F32)
    for hh in range(N_HEADS):
        is_head = head == hh
        p, den = _softmax2_rows(_dot_nt(jnp.where(is_head, q, jnp.zeros_like(q)), kn_ref[...]))
        yc = jnp.where(is_head, _dot_nt(p.astype(BF16), vnt_ref[...]) / den, yc)
        p, den = _softmax2_rows(_dot_nt(qm_ref[hh], km_ref[hh]))
        yd = jnp.where(is_head, _dot_nt(p.astype(BF16), vmt_ref[...]) / den, yd)
    yc_ref[...] = _rms(yc, gc_ref[...]).astype(BF16)
    yd_ref[...] = _rms(yd, gd_ref[...]).astype(BF16)


def _ctx_attention(qn, kn, vn, qm, km, vm, g_c, g_d, *, n_batch, seq, ctx_len):
    ctx0 = n_batch * seq // ctx_len
    row = pl.BlockSpec((ctx_len, GROUP_W), lambda b: (ctx0 + b, 0))
    heads = pl.BlockSpec((N_HEADS, ctx_len, HEAD_PAD), lambda b: (0, ctx0 + b, 0))
    cols = pl.BlockSpec((GROUP_W, ctx_len), lambda b: (0, ctx0 + b))
    gain = pl.BlockSpec((1, GROUP_W), lambda b: (0, 0))
    out = pl.BlockSpec((ctx_len, GROUP_W), lambda b: (b, 0))
    out_shape = jax.ShapeDtypeStruct((n_batch * ctx_len, GROUP_W), BF16)
    return pl.pallas_call(
        _ctx_body,
        grid=(n_batch,),
        in_specs=[row, row, cols, heads, heads, cols, gain, gain],
        out_specs=[out, out],
        out_shape=[out_shape, out_shape],
        compiler_params=_params(("arbitrary",)),
        name="ctx_attention",
    )(qn, kn, vn, qm, km, vm, g_c, g_d)


def _merge_body(x_ref, u_ref, up_ref, un_ref, bg_ref, yb_ref, yc_ref, yd_ref, ycc_ref, ydc_ref,
                cw_ref, ga_ref, wout_ref, mod_ref, n2g_ref, wrh_ref, wrl_ref, br_ref,
                xm_ref, h2_ref, ri_ref, rf_ref, cnt_ref, run_ref, *, t, n_lat, seq, ctx_len):
    i = pl.program_id(0)
    is_latent = i < n_lat
    yc = jnp.where(is_latent, yc_ref[...], ycc_ref[...])
    yd = jnp.where(is_latent, yd_ref[...], ydc_ref[...])

    seq_len = jnp.where(i < n_lat, seq, ctx_len)
    row = lax.broadcasted_iota(I32, (t, 1), 0)
    pos = lax.rem(i * t + row, seq_len)
    u = u_ref[...]
    u_prev = jnp.where(row == 0, up_ref[7:8, :], pltpu.roll(u, 1, axis=0))
    u_next = jnp.where(row == t - 1, un_ref[0:1, :], pltpu.roll(u, t - 1, axis=0))
    u_prev = jnp.where(pos == 0, 0.0, u_prev)
    u_next = jnp.where(pos == seq_len - 1, 0.0, u_next)
    cw = cw_ref[...]
    ya = bg_ref[...] * (u_prev * cw[0:1] + u * cw[1:2] + u_next * cw[2:3])
    ya = _rms(ya, ga_ref[...]).astype(BF16)

    merged = (_dot(ya, wout_ref[0:GROUP_W, :])
              + _dot(yb_ref[...], wout_ref[GROUP_W:2 * GROUP_W, :])
              + _dot(yc, wout_ref[2 * GROUP_W:3 * GROUP_W, :])
              + _dot(yd, wout_ref[3 * GROUP_W:4 * GROUP_W, :]))
    m = mod_ref[0]
    x_mid = x_ref[...] + m[2:3] * merged
    xm_ref[...] = x_mid
    h2 = _rms(x_mid, n2g_ref[...]) * (1.0 + m[4:5]) + m[3:4]
    h2_ref[...] = h2

    h_hi = h2.astype(BF16)
    h_lo = (h2 - h_hi.astype(F32)).astype(BF16)
    logits = (_dot(h_hi, wrh_ref[...]) + _dot(h_hi, wrl_ref[...]) + _dot(h_lo, wrh_ref[...])
              + br_ref[...])
    lane = lax.broadcasted_iota(I32, (t, LANES), 1)
    neg = jnp.float32(-jnp.inf)
    lg = jnp.where(lane < N_GROUPS, logits, neg)
    g_max = jnp.max(lg, axis=-1, keepdims=True)
    g_sel = jnp.min(jnp.where(lg == g_max, lane, LANES), axis=-1, keepdims=True)
    p_grp = 1.0 / jnp.sum(jnp.where(lane < N_GROUPS, jnp.exp(logits - g_max), 0.0), axis=-1, keepdims=True)
    first = N_GROUPS + EXPERTS_PER_GROUP * g_sel
    le = jnp.where((lane >= first) & (lane < first + EXPERTS_PER_GROUP), logits, neg)
    v1 = jnp.max(le, axis=-1, keepdims=True)
    i1 = jnp.min(jnp.where(le == v1, lane, LANES), axis=-1, keepdims=True)
    le2 = jnp.where(lane == i1, neg, le)
    v2 = jnp.max(le2, axis=-1, keepdims=True)
    i2 = jnp.min(jnp.where(le2 == v2, lane, LANES), axis=-1, keepdims=True)
    tt = jnp.exp(v2 - v1)
    gate0 = p_grp / (1.0 + tt)
    gate1 = p_grp * tt / (1.0 + tt)
    e0 = i1 - N_GROUPS
    e1 = i2 - N_GROUPS

    @pl.when(i == 0)
    def _():
        run_ref[...] = jnp.zeros_like(run_ref)

    sel0 = lane == e0
    sel1 = lane == e1
    onehot = jnp.where(sel0 | sel1, 1.0, 0.0)
    tri = jnp.where(lax.broadcasted_iota(I32, (t, t), 1) < lax.broadcasted_iota(I32, (t, t), 0), 1.0, 0.0)
    before = _dot(tri.astype(BF16), onehot.astype(BF16)) + run_ref[...]
    rank0 = jnp.sum(jnp.where(sel0, before, 0.0), axis=-1, keepdims=True).astype(I32)
    rank1 = jnp.sum(jnp.where(sel1, before, 0.0), axis=-1, keepdims=True).astype(I32)
    run_ref[...] = run_ref[...] + jnp.sum(onehot, axis=0, keepdims=True)
    cnt_ref[...] = jnp.broadcast_to(run_ref[...], cnt_ref.shape).astype(I32)

    ri = jnp.where(lane == 0, e0, jnp.where(lane == 1, e1, jnp.where(lane == 2, rank0, rank1)))
    ri_ref[0] = ri.T[0:8, :]
    rf_ref[...] = jnp.where(lane == 0, gate0, gate1)


def _merge(x_all, u, bg, yb, yc, yd, yc_ctx, yd_ctx, mod6, lw, *, t, n_run, n_lat, spt, n_batch,
           seq, ctx_len):
    nt, d = x_all.shape
    hb = t // 8
    n_halo = nt // 8
    latent = pl.BlockSpec((t, GROUP_W), lambda i: (jnp.minimum(i, n_lat - 1), 0))
    context = pl.BlockSpec((t, GROUP_W), lambda i: (jnp.maximum(i - n_lat, 0), 0))

    def mod_idx(i):
        return (jnp.where(i < n_lat, i // spt, n_batch), 0, 0)

    def full(a):
        return pl.BlockSpec(a.shape, lambda i, n=a.ndim: (0,) * n)

    row = lambda w: pl.BlockSpec((t, w), lambda i: (i, 0))
    weights_a = [lw["conv_w"], lw["g_a"], lw["w_out"]]
    weights_b = [lw["n2g"], lw["wr_hi"], lw["wr_lo"], lw["b_r"]]
    rows = n_run * t
    return pl.pallas_call(
        functools.partial(_merge_body, t=t, n_lat=n_lat, seq=seq, ctx_len=ctx_len),
        grid=(n_run,),
        in_specs=[row(d), row(GROUP_W),
                  pl.BlockSpec((8, GROUP_W), lambda i: (jnp.maximum(i * hb - 1, 0), 0)),
                  pl.BlockSpec((8, GROUP_W), lambda i: (jnp.minimum((i + 1) * hb, n_halo - 1), 0)),
                  row(GROUP_W), row(GROUP_W), latent, latent, context, context]
                 + [full(w) for w in weights_a] + [pl.BlockSpec((1, N_MOD, d), mod_idx)]
                 + [full(w) for w in weights_b],
        out_specs=[row(d), row(d), pl.BlockSpec((1, 8, t), lambda i: (i, 0, 0)), row(LANES),
                   pl.BlockSpec((8, LANES), lambda i: (0, 0))],
        out_shape=[jax.ShapeDtypeStruct((rows, d), F32), jax.ShapeDtypeStruct((rows, d), F32),
                   jax.ShapeDtypeStruct((n_run, 8, t), I32), jax.ShapeDtypeStruct((rows, LANES), F32),
                   jax.ShapeDtypeStruct((8, LANES), I32)],
        scratch_shapes=[pltpu.VMEM((1, LANES), F32)],
        compiler_params=_params(("arbitrary",)),
        name="merge",
    )(x_all, u, u, u, bg, yb, yc, yd, yc_ctx, yd_ctx, *weights_a, mod6, *weights_b)


def _row_copy(src, src_row, dst, dst_row, sem):
    return pltpu.make_async_copy(src.at[pl.ds(src_row, 1)], dst.at[pl.ds(dst_row, 1)], sem)


DMA_UNROLL = 8


def _dispatch_body(dest_ref, h_ref, zero_in, xb_ref, sem, *, t):
    del zero_in

    def issue(j, carry):
        _row_copy(h_ref, j, xb_ref, dest_ref[0, 0, j], sem).start()
        _row_copy(h_ref, j, xb_ref, dest_ref[0, 0, t + j], sem).start()
        return carry

    lax.fori_loop(0, t, issue, 0, unroll=DMA_UNROLL)
    for _ in range(2):
        pltpu.make_async_copy(h_ref, xb_ref.at[pl.ds(0, t)], sem).wait()


def _dispatch(dest, h2, cap, *, t):
    rows, d = h2.shape
    n_run = rows // t
    return pl.pallas_call(
        functools.partial(_dispatch_body, t=t),
        grid=(n_run,),
        in_specs=[pl.BlockSpec((1, 1, 2 * t), lambda i: (i, 0, 0), memory_space=pltpu.SMEM),
                  pl.BlockSpec((t, d), lambda i: (i, 0)),
                  pl.BlockSpec(memory_space=pl.ANY)],
        out_specs=pl.BlockSpec(memory_space=pl.ANY),
        out_shape=jax.ShapeDtypeStruct((cap, d), F32),
        input_output_aliases={2: 0},
        scratch_shapes=[pltpu.SemaphoreType.DMA],
        compiler_params=_params(("arbitrary",)),
        name="dispatch",
    )(dest.reshape(n_run, 1, 2 * t), h2, jnp.zeros((cap, d), F32))


def _experts_body(blk_ref, used_ref, x_ref, wg_ref, wu_ref, wd_ref, y_ref):
    in_use = pl.program_id(0) < used_ref[0]

    @pl.when(in_use)
    def _():
        xb = x_ref[...].astype(BF16)
        gate = _dot(xb, wg_ref[0].astype(BF16))
        up = _dot(xb, wu_ref[0].astype(BF16))
        act = (gate * jax.nn.sigmoid(gate) * up).astype(BF16)
        y_ref[...] = _dot(act, wd_ref[0].astype(BF16))

    @pl.when(jnp.logical_not(in_use))
    def _():
        y_ref[...] = jnp.zeros_like(y_ref)


def _experts(blk_expert, n_used, x_buf, w_gate, w_up, w_down):
    cap, d = x_buf.shape
    n_blk = cap // MOE_BLOCK

    def x_idx(b, blk, used):
        return (jnp.minimum(b, used[0] - 1), 0)

    def w_idx(b, blk, used):
        return (blk[jnp.minimum(b, used[0] - 1)], 0, 0)

    return pl.pallas_call(
        _experts_body,
        grid_spec=pltpu.PrefetchScalarGridSpec(
            num_scalar_prefetch=2,
            grid=(n_blk,),
            in_specs=[pl.BlockSpec((MOE_BLOCK, d), x_idx),
                      pl.BlockSpec((1, d, D_EXPERT), w_idx),
                      pl.BlockSpec((1, d, D_EXPERT), w_idx),
                      pl.BlockSpec((1, D_EXPERT, d), w_idx)],
            out_specs=pl.BlockSpec((MOE_BLOCK, d), lambda b, blk, used: (b, 0))),
        out_shape=jax.ShapeDtypeStruct((cap, d), F32),
        compiler_params=_params(("arbitrary",)),
        name="experts",
    )(blk_expert, n_used, x_buf, w_gate, w_up, w_down)


def _combine_body(dest_ref, x_ref, rf_ref, mod_ref, fg_ref, y_ref, o_ref, y0_ref, y1_ref, sem,
                  *, t, final):
    def issue(j, carry):
        _row_copy(y_ref, dest_ref[0, 0, j], y0_ref, j, sem).start()
        _row_copy(y_ref, dest_ref[0, 0, t + j], y1_ref, j, sem).start()
        return carry

    lax.fori_loop(0, t, issue, 0, unroll=DMA_UNROLL)
    for slot_ref in (y0_ref, y1_ref):
        pltpu.make_async_copy(y_ref.at[pl.ds(0, t)], slot_ref, sem).wait()
    rf = rf_ref[...]
    f = y0_ref[...] * rf[:, 0:1] + y1_ref[...] * rf[:, 1:2]
    x_new = x_ref[...] + mod_ref[0][5:6] * f
    o_ref[...] = _rms(x_new, fg_ref[...]) if final else x_new


def _combine(dest, x_mid, rf, mod6, final_g, y_buf, *, t, n_lat, spt, n_batch, final):
    rows, d = x_mid.shape
    n_run = rows // t

    def mod_idx(i):
        return (jnp.where(i < n_lat, i // spt, n_batch), 0, 0)

    return pl.pallas_call(
        functools.partial(_combine_body, t=t, final=final),
        grid=(n_run,),
        in_specs=[pl.BlockSpec((1, 1, 2 * t), lambda i: (i, 0, 0), memory_space=pltpu.SMEM),
                  pl.BlockSpec((t, d), lambda i: (i, 0)),
                  pl.BlockSpec((t, LANES), lambda i: (i, 0)),
                  pl.BlockSpec((1, N_MOD, d), mod_idx),
                  pl.BlockSpec((1, d), lambda i: (0, 0)),
                  pl.BlockSpec(memory_space=pl.ANY)],
        out_specs=pl.BlockSpec((t, d), lambda i: (i, 0)),
        out_shape=jax.ShapeDtypeStruct((rows, d), F32),
        scratch_shapes=[pltpu.VMEM((t, d), F32), pltpu.VMEM((t, d), F32), pltpu.SemaphoreType.DMA],
        compiler_params=_params(("arbitrary",)),
        name="combine",
    )(dest.reshape(n_run, 1, 2 * t), x_mid, rf, mod6, final_g, y_buf)


def _route_plan(ri, counts, cap):
    cnt = counts[0, :N_EXPERTS]
    padded = (cnt + MOE_BLOCK - 1) // MOE_BLOCK * MOE_BLOCK
    padded_end = jnp.cumsum(padded)
    padded_start = padded_end - padded
    experts = jnp.arange(N_EXPERTS, dtype=I32)

    def start_of(e):
        return jnp.sum(jnp.where(e[..., None] == experts, padded_start, 0), axis=-1)

    dest = jnp.concatenate([start_of(ri[:, 0]) + ri[:, 2], start_of(ri[:, 1]) + ri[:, 3]], axis=-1)
    n_blk = cap // MOE_BLOCK
    blk_start = jnp.arange(n_blk, dtype=I32) * MOE_BLOCK
    blk_expert = jnp.minimum(jnp.sum(blk_start[:, None] >= padded_end[None, :], axis=-1), N_EXPERTS - 1)
    n_used = (padded_end[-1:] // MOE_BLOCK).astype(I32)
    return dest.astype(I32)[:, None, :], blk_expert.astype(I32), n_used


def _rope_tables(seq, t):
    pos = jnp.arange(seq)
    row = (pos // GRID_W).astype(F32)
    col = (pos % GRID_W).astype(F32)
    n_freq = MLA_ROPE // 4
    inv_freq = ROPE_BASE ** (-jnp.arange(n_freq, dtype=F32) / n_freq)
    ang_r = row[:, None] * inv_freq
    ang_c = col[:, None] * inv_freq
    ang = jnp.concatenate([ang_r, ang_r, ang_c, ang_c], axis=-1)
    cos = jnp.concatenate([jnp.cos(ang), jnp.ones((t, MLA_ROPE), F32)], axis=0)
    sin = jnp.concatenate([jnp.sin(ang), jnp.zeros((t, MLA_ROPE), F32)], axis=0)
    n = seq + t
    pad = jnp.zeros((n, HEAD_PAD - MLA_NOPE - MLA_ROPE), F32)
    cos_q = jnp.concatenate([jnp.ones((n, MLA_NOPE), F32), cos, pad], axis=1)
    cos_k = jnp.concatenate([jnp.zeros((n, MLA_NOPE), F32), cos, pad], axis=1)
    sin_t = jnp.concatenate([jnp.zeros((n, MLA_NOPE), F32), sin, pad], axis=1)
    return cos_q, cos_k, sin_t


_ROT_SRC = np.concatenate([np.arange(8, 16), np.arange(0, 8), np.arange(24, 32), np.arange(16, 24)])
_ROT_SIGN = np.concatenate([-np.ones(8), np.ones(8), -np.ones(8), np.ones(8)]).astype(np.float32)


def _rot_cols(w):
    return w[..., _ROT_SRC] * _ROT_SIGN


def _layer_weights(p, l):
    d = p["w_in"].shape[1]
    w_in = p["w_in"][l]
    kr = w_in[:, KR0:KR0 + MLA_ROPE]
    zl = jnp.zeros((d, MLA_NOPE), F32)
    zr = jnp.zeros((d, HEAD_PAD - MLA_NOPE - MLA_ROPE), F32)
    w_in_ext = jnp.concatenate([w_in[:, :KR0], zl, kr, zr, zl, _rot_cols(kr), zr], axis=1)

    w_uq = p["mla_w_uq"][l].reshape(MLA_Q_RANK, N_HEADS, MLA_NOPE + MLA_ROPE)
    zq = jnp.zeros((MLA_Q_RANK, N_HEADS, HEAD_PAD - MLA_NOPE - MLA_ROPE), F32)
    q_plain = jnp.concatenate([w_uq, zq], axis=-1).reshape(MLA_Q_RANK, N_HEADS * HEAD_PAD)
    q_rot = jnp.concatenate([jnp.zeros((MLA_Q_RANK, N_HEADS, MLA_NOPE), F32),
                             _rot_cols(w_uq[..., MLA_NOPE:]), zq], axis=-1).reshape(MLA_Q_RANK, N_HEADS * HEAD_PAD)
    w_q = jnp.concatenate([q_plain, q_rot], axis=1)

    w_ukv = p["mla_w_ukv"][l].reshape(MLA_KV_RANK, N_HEADS, MLA_NOPE + HEAD_DIM)
    k_cols = jnp.concatenate([w_ukv[..., :MLA_NOPE], jnp.zeros((MLA_KV_RANK, N_HEADS, HEAD_PAD - MLA_NOPE), F32)],
                             axis=-1).reshape(MLA_KV_RANK, N_HEADS * HEAD_PAD)
    v_cols = w_ukv[..., MLA_NOPE:].reshape(MLA_KV_RANK, GROUP_W)
    w_kv = jnp.concatenate([k_cols, v_cols], axis=1)

    w_r = jnp.concatenate([p["w_grp"][l], p["w_exp"][l], jnp.zeros((d, LANES - N_GROUPS - N_EXPERTS), F32)], axis=1)
    wr_hi = w_r.astype(BF16)
    b_r = jnp.concatenate([p["b_grp"][l], p["b_exp"][l], jnp.zeros((LANES - N_GROUPS - N_EXPERTS,), F32)])[None]
    g_out = p["out_norm_g"][l]
    return dict(
        n1g=p["norm1_g"][l][None], n2g=p["norm2_g"][l][None], w_in=w_in_ext.astype(BF16),
        sg_w=p["sg_w"][l].astype(BF16), sg_b=jnp.repeat(p["sg_b"][l].T, HEAD_DIM, axis=1),
        g_a=g_out[None, 0:GROUP_W], g_b=g_out[None, GROUP_W:2 * GROUP_W],
        g_c=g_out[None, 2 * GROUP_W:3 * GROUP_W], g_d=g_out[None, 3 * GROUP_W:4 * GROUP_W],
        qng=p["mla_q_norm_g"][l][None], w_q=w_q.astype(BF16), kvg=p["mla_kv_norm_g"][l][None],
        w_kv=w_kv.astype(BF16), conv_w=p["conv_w"][l], w_out=p["w_out"][l].astype(BF16),
        wr_hi=wr_hi, wr_lo=(w_r - wr_hi.astype(F32)).astype(BF16), b_r=b_r)


def kernel(x, c, ctx, c_ctx, w_ada, b_ada, norm1_g, norm2_g, w_in, conv_w, sg_w, sg_b, na_rpb, mla_q_norm_g, mla_w_uq, mla_kv_norm_g, mla_w_ukv, out_norm_g, w_out, w_grp, b_grp, w_exp, b_exp, w_gate, w_up, w_down, final_norm_g):
    n_batch, seq, d = x.shape
    ctx_len = ctx.shape[1]
    n_layers = w_ada.shape[0]
    t = 512 if (n_batch * ctx_len) % 512 == 0 else 256
    assert seq % 512 == 0 and seq % ctx_len == 0 and t % ctx_len == 0 and n_batch < 8
    p = dict(w_in=w_in, mla_w_uq=mla_w_uq, mla_w_ukv=mla_w_ukv, w_grp=w_grp, w_exp=w_exp, b_grp=b_grp,
             b_exp=b_exp, out_norm_g=out_norm_g, norm1_g=norm1_g, norm2_g=norm2_g, sg_w=sg_w, sg_b=sg_b,
             mla_q_norm_g=mla_q_norm_g, mla_kv_norm_g=mla_kv_norm_g, conv_w=conv_w, w_out=w_out)

    n_lat = n_batch * seq // t
    spt = seq // t
    nt = n_batch * (seq + ctx_len)
    geo = dict(n_batch=n_batch, seq=seq, ctx_len=ctx_len)
    tile = dict(t=t, n_lat=n_lat, spt=spt, n_batch=n_batch)

    x_all = jnp.concatenate([x.reshape(n_batch * seq, d), ctx.reshape(n_batch * ctx_len, d)], axis=0)
    cc = jnp.concatenate([c, c_ctx[None], jnp.zeros((8 - n_batch - 1, d), F32)], axis=0)
    mod_all = _modulation(cc, w_ada, b_ada).reshape(n_layers, 8, N_MOD, d)
    tabs = _rope_tables(seq, t)
    final_g = final_norm_g[None]

    for l in range(n_layers):
        update_ctx = l < n_layers - 1
        lw = _layer_weights(p, l)
        mod6 = mod_all[l]
        u, bg, yb, qn, kn, vn, qm, km, vm = _proj(x_all, mod6, lw, tabs, **tile)
        bias = _nbr_bias_tables(na_rpb[l], seq // GRID_W)
        yc = _nbr_attention(qn, kn, vn, bias, lw["g_c"], **geo)
        yd = _mla_attention(qm, km, vm, lw["g_d"], **geo)
        if update_ctx:
            yc_ctx, yd_ctx = _ctx_attention(qn, kn, vn, qm, km, vm, lw["g_c"], lw["g_d"], **geo)
        else:
            yc_ctx = yd_ctx = jnp.zeros((n_batch * ctx_len, GROUP_W), BF16)
        n_run = nt // t if update_ctx else n_lat
        x_mid, h2, ri, rf, counts = _merge(x_all, u, bg, yb, yc, yd, yc_ctx, yd_ctx, mod6, lw, n_run=n_run,
                                           seq=seq, ctx_len=ctx_len, **tile)
        n_asg = 2 * n_run * t
        cap = (-(-n_asg // MOE_BLOCK) + N_EXPERTS) * MOE_BLOCK
        dest, blk_expert, n_used = _route_plan(ri, counts, cap)
        x_buf = _dispatch(dest, h2, cap, t=t)
        y_buf = _experts(blk_expert, n_used, x_buf, w_gate[l], w_up[l], w_down[l])
        x_all = _combine(dest, x_mid, rf, mod6, final_g, y_buf, final=not update_ctx, **tile)
    return x_all.reshape(n_batch, seq, d)
```

```python
import functools

import numpy as np
import jax
import jax.numpy as jnp
from jax import lax
from jax.experimental import pallas as pl
from jax.experimental.pallas import tpu as pltpu

F32 = jnp.float32
BF16 = jnp.bfloat16
I32 = jnp.int32

GRID_W = 64
HEAD_DIM = 64
N_HEADS = 4
GROUP_W = N_HEADS * HEAD_DIM
CONV_K = 3
SG_CHUNK = 128
NA_KH = 8
NA_KW = 16
MLA_NOPE = 64
MLA_ROPE = 32
MLA_Q_RANK = 256
MLA_KV_RANK = 128
MLA_SCALE = (MLA_NOPE + MLA_ROPE) ** -0.5
NA_SCALE = HEAD_DIM ** -0.5
ROPE_BASE = 10000.0
N_GROUPS = 4
EXPERTS_PER_GROUP = 8
N_EXPERTS = N_GROUPS * EXPERTS_PER_GROUP
D_EXPERT = 512
N_MOD = 6
EPS = 1e-6
LANES = 128
HEAD_PAD = 128
SUM_ROWS = 16
LOG2E = float(np.log2(np.e))
MOE_BLOCK = 256
NA_QROWS = 8
NA_KROWS = 16
VMEM_LIMIT = 56 * 1024 * 1024

SG0 = 3 * GROUP_W
NA0 = SG0 + 2 * GROUP_W
MLA0 = NA0 + 3 * GROUP_W
KR0 = MLA0 + MLA_Q_RANK + MLA_KV_RANK
W_IN_EXT = KR0 + 2 * HEAD_PAD


def _params(sem):
    return pltpu.CompilerParams(dimension_semantics=sem, vmem_limit_bytes=VMEM_LIMIT)


def _rms(x, g):
    return x * lax.rsqrt(jnp.mean(x * x, axis=-1, keepdims=True) + EPS) * g


def _dot(a, b):
    return jnp.dot(a, b, preferred_element_type=F32)


def _dot_nt(a, b):
    return lax.dot_general(a, b, (((1,), (1,)), ((), ())), preferred_element_type=F32)


def _head_of_lane(width):
    return lax.broadcasted_iota(I32, (1, width), 1) // HEAD_DIM


def _mod_body(c_ref, w_ref, b_ref, o_ref):
    c = c_ref[...]
    act = (c * jax.nn.sigmoid(c)).astype(BF16)
    o_ref[0] = _dot(act, w_ref[0].astype(BF16)) + b_ref[0]


def _modulation(cc, w_ada, b_ada):
    n_layers, d, n_out = w_ada.shape
    tn = 1536
    return pl.pallas_call(
        _mod_body,
        grid=(n_layers, n_out // tn),
        in_specs=[pl.BlockSpec((8, d), lambda l, j: (0, 0)),
                  pl.BlockSpec((1, d, tn), lambda l, j: (l, 0, j)),
                  pl.BlockSpec((1, 1, tn), lambda l, j: (l, 0, j))],
        out_specs=pl.BlockSpec((1, 8, tn), lambda l, j: (l, 0, j)),
        out_shape=jax.ShapeDtypeStruct((n_layers, 8, n_out), F32),
        compiler_params=_params(("arbitrary", "arbitrary")),
        name="modulation",
    )(cc, w_ada, b_ada.reshape(n_layers, 1, n_out))


def _gelu_tanh(x):
    cdf = 0.5 * (1.0 + jnp.tanh(np.float32(np.sqrt(2 / np.pi)) * (x + 0.044715 * (x * x * x))))
    return x * cdf


def _proj_body(x_ref, mod_ref, n1g_ref, win_ref, sgw_ref, sgb_ref, gb_ref, qng_ref, wq_ref,
               kvg_ref, wkv_ref, cq_ref, ck_ref, sn_ref,
               u_ref, bg_ref, yb_ref, qn_ref, kn_ref, vn_ref, qm_ref, km_ref, vm_ref):
    t = x_ref.shape[0]
    m = mod_ref[0]
    h = _rms(x_ref[...], n1g_ref[...]) * (1.0 + m[1:2]) + m[0:1]
    hb = h.astype(BF16)

    za = _dot(hb, win_ref[:, 0:SG0])
    bg_ref[...] = za[:, 0:GROUP_W]
    u_ref[...] = za[:, GROUP_W:2 * GROUP_W] * za[:, 2 * GROUP_W:3 * GROUP_W]

    g = _gelu_tanh(_dot(hb, win_ref[:, SG0:NA0]))
    ub, vb = g[:, 0:GROUP_W], g[:, GROUP_W:2 * GROUP_W]
    mu = jnp.mean(vb, axis=-1, keepdims=True)
    vc = vb - mu
    vn = (vc * lax.rsqrt(jnp.mean(vc * vc, axis=-1, keepdims=True) + EPS)).astype(BF16)
    head = _head_of_lane(GROUP_W)
    mixed = []
    for c in range(t // SG_CHUNK):
        vch = vn[c * SG_CHUNK:(c + 1) * SG_CHUNK]
        acc = sgb_ref[...]
        for hh in range(N_HEADS):
            acc = acc + _dot(sgw_ref[hh], jnp.where(head == hh, vch, jnp.zeros_like(vch)))
        mixed.append(acc)
    yb = ub * jnp.concatenate(mixed, axis=0)
    yb_ref[...] = _rms(yb, gb_ref[...]).astype(BF16)

    zc = _dot(hb, win_ref[:, NA0:MLA0])
    qn_ref[...] = (zc[:, 0:GROUP_W] * (NA_SCALE * LOG2E)).astype(BF16)
    kn_ref[...] = zc[:, GROUP_W:2 * GROUP_W].astype(BF16)
    vn_ref[...] = zc[:, 2 * GROUP_W:3 * GROUP_W].T.astype(BF16)

    zd = _dot(hb, win_ref[:, MLA0:W_IN_EXT])
    cq = zd[:, 0:MLA_Q_RANK]
    ckv = zd[:, MLA_Q_RANK:MLA_Q_RANK + MLA_KV_RANK]
    kr = zd[:, MLA_Q_RANK + MLA_KV_RANK:MLA_Q_RANK + MLA_KV_RANK + HEAD_PAD]
    kr_rot = zd[:, MLA_Q_RANK + MLA_KV_RANK + HEAD_PAD:]
    qq = _dot(_rms(cq, qng_ref[...]).astype(BF16), wq_ref[...])
    kk = _dot(_rms(ckv, kvg_ref[...]).astype(BF16), wkv_ref[...])
    cos_q, cos_k, sin = cq_ref[...], ck_ref[...], sn_ref[...]
    kr_roped = kr * cos_k + kr_rot * sin
    wq_half = N_HEADS * HEAD_PAD
    for hh in range(N_HEADS):
        lo = hh * HEAD_PAD
        q_h = qq[:, lo:lo + HEAD_PAD] * cos_q + qq[:, wq_half + lo:wq_half + lo + HEAD_PAD] * sin
        qm_ref[hh] = (q_h * (MLA_SCALE * LOG2E)).astype(BF16)
        km_ref[hh] = (kk[:, lo:lo + HEAD_PAD] + kr_roped).astype(BF16)
    vm_ref[...] = kk[:, wq_half:wq_half + GROUP_W].T.astype(BF16)


def _proj(x_all, mod6, lw, tabs, *, t, n_lat, spt, n_batch):
    nt, d = x_all.shape

    def mod_idx(i):
        return (jnp.where(i < n_lat, i // spt, n_batch), 0, 0)

    def tab_idx(i):
        return (jnp.where(i < n_lat, i % spt, spt), 0)

    def full(a):
        return pl.BlockSpec(a.shape, lambda i, n=a.ndim: (0,) * n)

    row = lambda w: pl.BlockSpec((t, w), lambda i: (i, 0))
    heads = pl.BlockSpec((N_HEADS, t, HEAD_PAD), lambda i: (0, i, 0))
    tab = pl.BlockSpec((t, HEAD_PAD), tab_idx)
    weights = [lw["n1g"], lw["w_in"], lw["sg_w"], lw["sg_b"], lw["g_b"], lw["qng"], lw["w_q"],
               lw["kvg"], lw["w_kv"]]
    cols = pl.BlockSpec((GROUP_W, t), lambda i: (0, i))
    out_shape = ([jax.ShapeDtypeStruct((nt, GROUP_W), F32)] * 2
                 + [jax.ShapeDtypeStruct((nt, GROUP_W), BF16)] * 3
                 + [jax.ShapeDtypeStruct((GROUP_W, nt), BF16)]
                 + [jax.ShapeDtypeStruct((N_HEADS, nt, HEAD_PAD), BF16)] * 2
                 + [jax.ShapeDtypeStruct((GROUP_W, nt), BF16)])
    return pl.pallas_call(
        _proj_body,
        grid=(nt // t,),
        in_specs=[row(d), pl.BlockSpec((1, N_MOD, d), mod_idx)] + [full(w) for w in weights]
                 + [tab, tab, tab],
        out_specs=[row(GROUP_W)] * 5 + [cols, heads, heads, cols],
        out_shape=out_shape,
        compiler_params=_params(("arbitrary",)),
        name="proj",
    )(x_all, mod6, *weights, *tabs)


def _nbr_body(q_ref, k_ref, vt_ref, kc_ref, vtc_ref, bias_ref, g_ref, o_ref, s_ref, *, n_rows):
    i = pl.program_id(1)
    start_row = jnp.clip(i * NA_QROWS - NA_KH // 2, 0, n_rows - NA_KROWS)
    start = pl.multiple_of(start_row * GRID_W, GRID_W * 4)
    n_loc = NA_KROWS * GRID_W
    n_keys = s_ref.shape[1]
    kt = k_ref[pl.ds(start, n_loc), :]
    kc = kc_ref[...]
    q = q_ref[...]
    head = _head_of_lane(GROUP_W)
    ones = jnp.ones((SUM_ROWS, n_keys), BF16)

    def scores(hh):
        qh = jnp.where(head == hh, q, jnp.zeros_like(q))
        s_ref[hh % 2, 0:n_loc, :] = _dot_nt(kt, qh) + bias_ref[0, hh]
        s_ref[hh % 2, n_loc:n_keys, :] = _dot_nt(kc, qh)

    def attend(hh):
        rows = pl.ds(hh * HEAD_DIM, HEAD_DIM)
        s = s_ref[hh % 2]
        p = jnp.exp2(s - jnp.max(s, axis=0, keepdims=True)).astype(BF16)
        values_t = jnp.concatenate([vt_ref[rows, pl.ds(start, n_loc)], vtc_ref[rows, :]], axis=1)
        a = _dot(jnp.concatenate([values_t, ones], axis=0), p)
        return a[0:HEAD_DIM] / a[HEAD_DIM:HEAD_DIM + 1]

    heads = []
    scores(0)
    for hh in range(N_HEADS):
        if hh + 1 < N_HEADS:
            scores(hh + 1)
        heads.append(attend(hh))
    o = jnp.concatenate(heads, axis=0)
    y = o * lax.rsqrt(jnp.mean(o * o, axis=0, keepdims=True) + EPS) * g_ref[...]
    o_ref[...] = y.T.astype(BF16)


def _nbr_attention(qn, kn, vnt, bias, g_c, *, n_batch, seq, ctx_len):
    tq = NA_QROWS * GRID_W
    tiles = seq // tq
    n_rows = seq // GRID_W
    ctx0 = n_batch * seq // ctx_len

    def bias_idx(b, i):
        return (jnp.where(i == 0, 0, jnp.where(i == tiles - 1, 2, 1)), 0, 0, 0)

    return pl.pallas_call(
        functools.partial(_nbr_body, n_rows=n_rows),
        grid=(n_batch, tiles),
        in_specs=[pl.BlockSpec((tq, GROUP_W), lambda b, i: (b * tiles + i, 0)),
                  pl.BlockSpec((seq, GROUP_W), lambda b, i: (b, 0)),
                  pl.BlockSpec((GROUP_W, seq), lambda b, i: (0, b)),
                  pl.BlockSpec((ctx_len, GROUP_W), lambda b, i: (ctx0 + b, 0)),
                  pl.BlockSpec((GROUP_W, ctx_len), lambda b, i: (0, ctx0 + b)),
                  pl.BlockSpec((1, N_HEADS, NA_KROWS * GRID_W, tq), bias_idx),
                  pl.BlockSpec((GROUP_W, 1), lambda b, i: (0, 0))],
        out_specs=pl.BlockSpec((tq, GROUP_W), lambda b, i: (b * tiles + i, 0)),
        out_shape=jax.ShapeDtypeStruct((n_batch * seq, GROUP_W), BF16),
        scratch_shapes=[pltpu.VMEM((2, NA_KROWS * GRID_W + ctx_len, tq), F32)],
        compiler_params=_params(("arbitrary", "arbitrary")),
        name="nbr_attention",
    )(qn, kn, vnt, kn, vnt, bias, g_c.reshape(GROUP_W, 1))


def _nbr_bias_tables(rpb, n_rows):
    assert n_rows >= NA_KROWS
    col = np.arange(GRID_W)
    c0 = np.clip(col - NA_KW // 2, 0, GRID_W - NA_KW)
    in_win = (col[:, None] >= c0[None, :]) & (col[:, None] < c0[None, :] + NA_KW)
    dc = np.clip(col[:, None] - col[None, :] + (NA_KW - 1), 0, 2 * NA_KW - 2)
    blocks = jnp.where(in_win[None, None], rpb[:, :, dc] * LOG2E, -jnp.inf)
    masked = jnp.full((rpb.shape[0], GRID_W, GRID_W), -jnp.inf, F32)
    tables = []
    for r0, st in ((0, 0), (NA_QROWS, NA_QROWS - NA_KH // 2), (n_rows - NA_QROWS, n_rows - NA_KROWS)):
        strips = []
        for j in range(NA_KROWS):
            strip = []
            for a in range(NA_QROWS):
                r = r0 + a
                lo = min(max(r - NA_KH // 2, 0), n_rows - NA_KH)
                strip.append(blocks[:, st + j - r + NA_KH - 1] if lo <= st + j < lo + NA_KH else masked)
            strips.append(jnp.concatenate(strip, axis=-1))
        tables.append(jnp.concatenate(strips, axis=1))
    return jnp.stack(tables)


def _mla_body(q_ref, k_ref, vt_ref, kc_ref, vtc_ref, g_ref, o_ref, m_ref, acc_ref, s_ref,
              *, tk, n_chunks):
    m_ref[...] = jnp.full_like(m_ref, -jnp.inf)
    acc_ref[...] = jnp.zeros_like(acc_ref)

    def absorb(hh, s, values_t):
        m_prev = m_ref[hh]
        m_new = jnp.maximum(m_prev, jnp.max(s, axis=0, keepdims=True))
        p = jnp.exp2(s - m_new).astype(BF16)
        lhs = jnp.concatenate([values_t, jnp.ones((SUM_ROWS, s.shape[0]), BF16)], axis=0)
        acc_ref[hh] = jnp.exp2(m_prev - m_new) * acc_ref[hh] + _dot(lhs, p)
        m_ref[hh] = m_new

    def scores(hh, off):
        return _dot_nt(k_ref[hh, pl.ds(off, tk), :], q_ref[hh])

    s_ref[0] = scores(0, 0)

    def chunk(j, carry):
        off = pl.multiple_of(j * tk, tk)
        off_next = pl.multiple_of(jnp.minimum(j + 1, n_chunks - 1) * tk, tk)
        for hh in range(N_HEADS):
            if hh + 1 < N_HEADS:
                s_ref[(hh + 1) % 2] = scores(hh + 1, off)
            else:
                s_ref[(hh + 1) % 2] = scores(0, off_next)
            absorb(hh, s_ref[hh % 2], vt_ref[pl.ds(hh * HEAD_DIM, HEAD_DIM), pl.ds(off, tk)])
        return carry

    lax.fori_loop(0, n_chunks, chunk, 0, unroll=2)
    for hh in range(N_HEADS):
        absorb(hh, _dot_nt(kc_ref[hh], q_ref[hh]), vtc_ref[pl.ds(hh * HEAD_DIM, HEAD_DIM), :])

    heads = []
    for hh in range(N_HEADS):
        a = acc_ref[hh]
        heads.append(a[0:HEAD_DIM] / a[HEAD_DIM:HEAD_DIM + 1])
    o = jnp.concatenate(heads, axis=0)
    y = o * lax.rsqrt(jnp.mean(o * o, axis=0, keepdims=True) + EPS) * g_ref[...]
    o_ref[...] = y.T.astype(BF16)


def _mla_attention(qm, km, vmt, g_d, *, n_batch, seq, ctx_len):
    tq = 512
    tk = 512
    tiles = seq // tq
    ctx0 = n_batch * seq // ctx_len
    return pl.pallas_call(
        functools.partial(_mla_body, tk=tk, n_chunks=seq // tk),
        grid=(n_batch, tiles),
        in_specs=[pl.BlockSpec((N_HEADS, tq, HEAD_PAD), lambda b, i: (0, b * tiles + i, 0)),
                  pl.BlockSpec((N_HEADS, seq, HEAD_PAD), lambda b, i: (0, b, 0)),
                  pl.BlockSpec((GROUP_W, seq), lambda b, i: (0, b)),
                  pl.BlockSpec((N_HEADS, ctx_len, HEAD_PAD), lambda b, i: (0, ctx0 + b, 0)),
                  pl.BlockSpec((GROUP_W, ctx_len), lambda b, i: (0, ctx0 + b)),
                  pl.BlockSpec((GROUP_W, 1), lambda b, i: (0, 0))],
        out_specs=pl.BlockSpec((tq, GROUP_W), lambda b, i: (b * tiles + i, 0)),
        out_shape=jax.ShapeDtypeStruct((n_batch * seq, GROUP_W), BF16),
        scratch_shapes=[pltpu.VMEM((N_HEADS, 1, tq), F32),
                        pltpu.VMEM((N_HEADS, HEAD_DIM + SUM_ROWS, tq), F32),
                        pltpu.VMEM((2, tk, tq), F32)],
        compiler_params=_params(("arbitrary", "arbitrary")),
        name="mla_attention",
    )(qm, km, vmt, km, vmt, g_d.reshape(GROUP_W, 1))


def _softmax2_rows(s):
    p = jnp.exp2(s - jnp.max(s, axis=-1, keepdims=True))
    return p, jnp.sum(p, axis=-1, keepdims=True)


def _ctx_body(qn_ref, kn_ref, vnt_ref, qm_ref, km_ref, vmt_ref, gc_ref, gd_ref, yc_ref, yd_ref):
    head = _head_of_lane(GROUP_W)
    q = qn_ref[...]
    yc = jnp.zeros(q.shape, F32)
    yd = jnp.zeros(q.shape, F32)
    for hh in range(N_HEADS):
        is_head = head == hh
        p, den = _softmax2_rows(_dot_nt(jnp.where(is_head, q, jnp.zeros_like(q)), kn_ref[...]))
        yc = jnp.where(is_head, _dot_nt(p.astype(BF16), vnt_ref[...]) / den, yc)
        p, den = _softmax2_rows(_dot_nt(qm_ref[hh], km_ref[hh]))
        yd = jnp.where(is_head, _dot_nt(p.astype(BF16), vmt_ref[...]) / den, yd)
    yc_ref[...] = _rms(yc, gc_ref[...]).astype(BF16)
    yd_ref[...] = _rms(yd, gd_ref[...]).astype(BF16)


def _ctx_attention(qn, kn, vn, qm, km, vm, g_c, g_d, *, n_batch, seq, ctx_len):
    ctx0 = n_batch * seq // ctx_len
    row = pl.BlockSpec((ctx_len, GROUP_W), lambda b: (ctx0 + b, 0))
    heads = pl.BlockSpec((N_HEADS, ctx_len, HEAD_PAD), lambda b: (0, ctx0 + b, 0))
    cols = pl.BlockSpec((GROUP_W, ctx_len), lambda b: (0, ctx0 + b))
    gain = pl.BlockSpec((1, GROUP_W), lambda b: (0, 0))
    out = pl.BlockSpec((ctx_len, GROUP_W), lambda b: (b, 0))
    out_shape = jax.ShapeDtypeStruct((n_batch * ctx_len, GROUP_W), BF16)
    return pl.pallas_call(
        _ctx_body,
        grid=(n_batch,),
        in_specs=[row, row, cols, heads, heads, cols, gain, gain],
        out_specs=[out, out],
        out_shape=[out_shape, out_shape],
        compiler_params=_params(("arbitrary",)),
        name="ctx_attention",
    )(qn, kn, vn, qm, km, vm, g_c, g_d)


def _merge_body(x_ref, u_ref, up_ref, un_ref, bg_ref, yb_ref, yc_ref, yd_ref, ycc_ref, ydc_ref,
                cw_ref, ga_ref, wout_ref, mod_ref, n2g_ref, wrh_ref, wrl_ref, br_ref,
                xm_ref, h2_ref, ri_ref, rf_ref, cnt_ref, run_ref, *, t, n_lat, seq, ctx_len):
    i = pl.program_id(0)
    is_latent = i < n_lat
    yc = jnp.where(is_latent, yc_ref[...], ycc_ref[...])
    yd = jnp.where(is_latent, yd_ref[...], ydc_ref[...])

    row = lax.broadcasted_iota(I32, (t, 1), 0)
    tile_pos = lax.rem(i * t, seq)
    ctx_first = functools.reduce(jnp.logical_or, [row == k for k in range(0, t, ctx_len)])
    ctx_last = functools.reduce(jnp.logical_or, [row == k + ctx_len - 1 for k in range(0, t, ctx_len)])
    as_int = lambda mask: jnp.where(mask, 1, 0)
    seq_first = jnp.where(is_latent, as_int((row == 0) & (tile_pos == 0)), as_int(ctx_first)) == 1
    seq_last = jnp.where(is_latent, as_int((row == t - 1) & (tile_pos == seq - t)), as_int(ctx_last)) == 1
    u = u_ref[...]
    u_prev = jnp.where(row == 0, up_ref[7:8, :], pltpu.roll(u, 1, axis=0))
    u_next = jnp.where(row == t - 1, un_ref[0:1, :], pltpu.roll(u, t - 1, axis=0))
    u_prev = jnp.where(seq_first, 0.0, u_prev)
    u_next = jnp.where(seq_last, 0.0, u_next)
    cw = cw_ref[...]
    ya = bg_ref[...] * (u_prev * cw[0:1] + u * cw[1:2] + u_next * cw[2:3])
    ya = _rms(ya, ga_ref[...]).astype(BF16)

    merged = (_dot(ya, wout_ref[0:GROUP_W, :])
              + _dot(yb_ref[...], wout_ref[GROUP_W:2 * GROUP_W, :])
              + _dot(yc, wout_ref[2 * GROUP_W:3 * GROUP_W, :])
              + _dot(yd, wout_ref[3 * GROUP_W:4 * GROUP_W, :]))
    m = mod_ref[0]
    x_mid = x_ref[...] + m[2:3] * merged
    xm_ref[...] = x_mid
    h2 = _rms(x_mid, n2g_ref[...]) * (1.0 + m[4:5]) + m[3:4]
    h2_ref[...] = h2

    h_hi = h2.astype(BF16)
    h_lo = (h2 - h_hi.astype(F32)).astype(BF16)
    logits = (_dot(h_hi, wrh_ref[...]) + _dot(h_hi, wrl_ref[...]) + _dot(h_lo, wrh_ref[...])
              + br_ref[...])
    lane = lax.broadcasted_iota(I32, (t, LANES), 1)
    neg = jnp.float32(-jnp.inf)
    lg = jnp.where(lane < N_GROUPS, logits, neg)
    g_max = jnp.max(lg, axis=-1, keepdims=True)
    g_sel = jnp.min(jnp.where(lg == g_max, lane, LANES), axis=-1, keepdims=True)
    p_grp = 1.0 / jnp.sum(jnp.where(lane < N_GROUPS, jnp.exp(logits - g_max), 0.0), axis=-1, keepdims=True)
    first = N_GROUPS + EXPERTS_PER_GROUP * g_sel
    le = jnp.where((lane >= first) & (lane < first + EXPERTS_PER_GROUP), logits, neg)
    v1 = jnp.max(le, axis=-1, keepdims=True)
    i1 = jnp.min(jnp.where(le == v1, lane, LANES), axis=-1, keepdims=True)
    le2 = jnp.where(lane == i1, neg, le)
    v2 = jnp.max(le2, axis=-1, keepdims=True)
    i2 = jnp.min(jnp.where(le2 == v2, lane, LANES), axis=-1, keepdims=True)
    tt = jnp.exp(v2 - v1)
    gate0 = p_grp / (1.0 + tt)
    gate1 = p_grp * tt / (1.0 + tt)
    e0 = i1 - N_GROUPS
    e1 = i2 - N_GROUPS

    @pl.when(i == 0)
    def _():
        run_ref[...] = jnp.zeros_like(run_ref)

    sel0 = lane == e0
    sel1 = lane == e1
    onehot = jnp.where(sel0 | sel1, 1.0, 0.0)
    tri = jnp.where(lax.broadcasted_iota(I32, (t, t), 1) < lax.broadcasted_iota(I32, (t, t), 0), 1.0, 0.0)
    before = _dot(tri.astype(BF16), onehot.astype(BF16)) + run_ref[...]
    rank0 = jnp.sum(jnp.where(sel0, before, 0.0), axis=-1, keepdims=True).astype(I32)
    rank1 = jnp.sum(jnp.where(sel1, before, 0.0), axis=-1, keepdims=True).astype(I32)
    run_ref[...] = run_ref[...] + jnp.sum(onehot, axis=0, keepdims=True)
    cnt_ref[...] = jnp.broadcast_to(run_ref[...], cnt_ref.shape).astype(I32)

    ri = jnp.where(lane == 0, e0, jnp.where(lane == 1, e1, jnp.where(lane == 2, rank0, rank1)))
    ri_ref[0] = ri.T[0:8, :]
    rf_ref[...] = jnp.where(lane == 0, gate0, gate1)


def _merge(x_all, u, bg, yb, yc, yd, yc_ctx, yd_ctx, mod6, lw, *, t, n_run, n_lat, spt, n_batch,
           seq, ctx_len):
    nt, d = x_all.shape
    hb = t // 8
    n_halo = nt // 8
    latent = pl.BlockSpec((t, GROUP_W), lambda i: (jnp.minimum(i, n_lat - 1), 0))
    context = pl.BlockSpec((t, GROUP_W), lambda i: (jnp.maximum(i - n_lat, 0), 0))

    def mod_idx(i):
        return (jnp.where(i < n_lat, i // spt, n_batch), 0, 0)

    def full(a):
        return pl.BlockSpec(a.shape, lambda i, n=a.ndim: (0,) * n)

    row = lambda w: pl.BlockSpec((t, w), lambda i: (i, 0))
    weights_a = [lw["conv_w"], lw["g_a"], lw["w_out"]]
    weights_b = [lw["n2g"], lw["wr_hi"], lw["wr_lo"], lw["b_r"]]
    rows = n_run * t
    return pl.pallas_call(
        functools.partial(_merge_body, t=t, n_lat=n_lat, seq=seq, ctx_len=ctx_len),
        grid=(n_run,),
        in_specs=[row(d), row(GROUP_W),
                  pl.BlockSpec((8, GROUP_W), lambda i: (jnp.maximum(i * hb - 1, 0), 0)),
                  pl.BlockSpec((8, GROUP_W), lambda i: (jnp.minimum((i + 1) * hb, n_halo - 1), 0)),
                  row(GROUP_W), row(GROUP_W), latent, latent, context, context]
                 + [full(w) for w in weights_a] + [pl.BlockSpec((1, N_MOD, d), mod_idx)]
                 + [full(w) for w in weights_b],
        out_specs=[row(d), row(d), pl.BlockSpec((1, 8, t), lambda i: (i, 0, 0)), row(LANES),
                   pl.BlockSpec((8, LANES), lambda i: (0, 0))],
        out_shape=[jax.ShapeDtypeStruct((rows, d), F32), jax.ShapeDtypeStruct((rows, d), F32),
                   jax.ShapeDtypeStruct((n_run, 8, t), I32), jax.ShapeDtypeStruct((rows, LANES), F32),
                   jax.ShapeDtypeStruct((8, LANES), I32)],
        scratch_shapes=[pltpu.VMEM((1, LANES), F32)],
        compiler_params=_params(("arbitrary",)),
        name="merge",
    )(x_all, u, u, u, bg, yb, yc, yd, yc_ctx, yd_ctx, *weights_a, mod6, *weights_b)


def _row_copy(src, src_row, dst, dst_row, sem):
    return pltpu.make_async_copy(src.at[pl.ds(src_row, 1)], dst.at[pl.ds(dst_row, 1)], sem)


DMA_UNROLL = 8


def _dispatch_body(pad_start_ref, pad_len_ref, used_ref, dest_ref, h_ref, xb_ref, zero_ref, sem,
                   pad_sem, *, t, experts_per_step, blocks_per_step):
    def issue(g, carry):
        base = pl.multiple_of(g * DMA_UNROLL, DMA_UNROLL)
        for k in range(DMA_UNROLL):
            _row_copy(h_ref, base + k, xb_ref, dest_ref[0, 0, base + k], sem).start(priority=0)
            _row_copy(h_ref, base + k, xb_ref, dest_ref[0, 0, t + base + k], sem).start(priority=1)
        return carry

    lax.fori_loop(0, t // DMA_UNROLL, issue, 0)

    zero_ref[...] = jnp.zeros_like(zero_ref)
    for k in range(experts_per_step):
        e = pl.program_id(0) * experts_per_step + k
        in_range = e < N_EXPERTS
        e = jnp.minimum(e, N_EXPERTS - 1)
        n_pad = jnp.where(in_range, pad_len_ref[e], 0)
        first = pad_start_ref[e]

        def fill(r, carry, first=first):
            _row_copy(zero_ref, 0, xb_ref, first + r, pad_sem).start()
            return carry

        def drain(r, carry):
            _row_copy(zero_ref, 0, xb_ref, 0, pad_sem).wait()
            return carry

        lax.fori_loop(0, n_pad, fill, 0)
        lax.fori_loop(0, n_pad, drain, 0)

    n_blk = xb_ref.shape[0] // MOE_BLOCK
    for k in range(blocks_per_step):
        b = used_ref[0] + pl.program_id(0) * blocks_per_step + k

        @pl.when(b < n_blk)
        def _(b=b):
            tail = pltpu.make_async_copy(
                zero_ref, xb_ref.at[pl.ds(pl.multiple_of(b * MOE_BLOCK, MOE_BLOCK), MOE_BLOCK)], pad_sem)
            tail.start()
            tail.wait()

    for _ in range(2):
        pltpu.make_async_copy(h_ref, xb_ref.at[pl.ds(0, t)], sem).wait()


def _dispatch(dest, pad_start, pad_len, n_used, h2, cap, *, t):
    rows, d = h2.shape
    n_run = rows // t
    experts_per_step = -(-N_EXPERTS // n_run)
    blocks_per_step = -(-N_EXPERTS // n_run)
    return pl.pallas_call(
        functools.partial(_dispatch_body, t=t, experts_per_step=experts_per_step,
                          blocks_per_step=blocks_per_step),
        grid_spec=pltpu.PrefetchScalarGridSpec(
            num_scalar_prefetch=3,
            grid=(n_run,),
            in_specs=[pl.BlockSpec((1, 1, 2 * t), lambda i, *_: (i, 0, 0), memory_space=pltpu.SMEM),
                      pl.BlockSpec((t, d), lambda i, *_: (i, 0))],
            out_specs=pl.BlockSpec(memory_space=pl.ANY),
            scratch_shapes=[pltpu.VMEM((MOE_BLOCK, d), F32), pltpu.SemaphoreType.DMA,
                            pltpu.SemaphoreType.DMA]),
        out_shape=jax.ShapeDtypeStruct((cap, d), F32),
        compiler_params=_params(("arbitrary",)),
        name="dispatch",
    )(pad_start, pad_len, n_used, dest, h2)


def _experts_body(blk_ref, used_ref, x_ref, wg_ref, wu_ref, wd_ref, y_ref):
    in_use = pl.program_id(0) < used_ref[0]

    @pl.when(in_use)
    def _():
        xb = x_ref[...].astype(BF16)
        gate = _dot(xb, wg_ref[0, 0].astype(BF16))
        up = _dot(xb, wu_ref[0, 0].astype(BF16))
        act = (gate * jax.nn.sigmoid(gate) * up).astype(BF16)
        y_ref[...] = _dot(act, wd_ref[0, 0].astype(BF16))

    @pl.when(jnp.logical_not(in_use))
    def _():
        y_ref[...] = jnp.zeros_like(y_ref)


def _experts(blk_expert, n_used, x_buf, w_gate, w_up, w_down, layer):
    cap, d = x_buf.shape
    n_blk = cap // MOE_BLOCK

    def x_idx(b, blk, used):
        return (jnp.minimum(b, used[0] - 1), 0)

    def w_idx(b, blk, used):
        return (layer, blk[jnp.minimum(b, used[0] - 1)], 0, 0)

    return pl.pallas_call(
        _experts_body,
        grid_spec=pltpu.PrefetchScalarGridSpec(
            num_scalar_prefetch=2,
            grid=(n_blk,),
            in_specs=[pl.BlockSpec((MOE_BLOCK, d), x_idx),
                      pl.BlockSpec((1, 1, d, D_EXPERT), w_idx),
                      pl.BlockSpec((1, 1, d, D_EXPERT), w_idx),
                      pl.BlockSpec((1, 1, D_EXPERT, d), w_idx)],
            out_specs=pl.BlockSpec((MOE_BLOCK, d), lambda b, blk, used: (b, 0))),
        out_shape=jax.ShapeDtypeStruct((cap, d), F32),
        compiler_params=_params(("arbitrary",)),
        name="experts",
    )(blk_expert, n_used, x_buf, w_gate, w_up, w_down)


def _combine_body(dest_ref, x_ref, rf_ref, mod_ref, fg_ref, y_ref, o_ref, y0_ref, y1_ref, sem,
                  *, t, final):
    def issue(g, carry):
        base = pl.multiple_of(g * DMA_UNROLL, DMA_UNROLL)
        for k in range(DMA_UNROLL):
            _row_copy(y_ref, dest_ref[0, 0, base + k], y0_ref, base + k, sem).start(priority=0)
            _row_copy(y_ref, dest_ref[0, 0, t + base + k], y1_ref, base + k, sem).start(priority=1)
        return carry

    lax.fori_loop(0, t // DMA_UNROLL, issue, 0)
    for slot_ref in (y0_ref, y1_ref):
        pltpu.make_async_copy(y_ref.at[pl.ds(0, t)], slot_ref, sem).wait()
    rf = rf_ref[...]
    f = y0_ref[...] * rf[:, 0:1] + y1_ref[...] * rf[:, 1:2]
    x_new = x_ref[...] + mod_ref[0][5:6] * f
    o_ref[...] = _rms(x_new, fg_ref[...]) if final else x_new


def _combine(dest, x_mid, rf, mod6, final_g, y_buf, *, t, n_lat, spt, n_batch, final):
    rows, d = x_mid.shape
    n_run = rows // t

    def mod_idx(i):
        return (jnp.where(i < n_lat, i // spt, n_batch), 0, 0)

    return pl.pallas_call(
        functools.partial(_combine_body, t=t, final=final),
        grid=(n_run,),
        in_specs=[pl.BlockSpec((1, 1, 2 * t), lambda i: (i, 0, 0), memory_space=pltpu.SMEM),
                  pl.BlockSpec((t, d), lambda i: (i, 0)),
                  pl.BlockSpec((t, LANES), lambda i: (i, 0)),
                  pl.BlockSpec((1, N_MOD, d), mod_idx),
                  pl.BlockSpec((1, d), lambda i: (0, 0)),
                  pl.BlockSpec(memory_space=pl.ANY)],
        out_specs=pl.BlockSpec((t, d), lambda i: (i, 0)),
        out_shape=jax.ShapeDtypeStruct((rows, d), F32),
        scratch_shapes=[pltpu.VMEM((t, d), F32), pltpu.VMEM((t, d), F32), pltpu.SemaphoreType.DMA],
        compiler_params=_params(("arbitrary",)),
        name="combine",
    )(dest.reshape(n_run, 1, 2 * t), x_mid, rf, mod6, final_g, y_buf)


def _route_plan(ri, counts, cap):
    cnt = counts[0, :N_EXPERTS]
    padded = (cnt + MOE_BLOCK - 1) // MOE_BLOCK * MOE_BLOCK
    padded_end = jnp.cumsum(padded)
    padded_start = padded_end - padded
    experts = jnp.arange(N_EXPERTS, dtype=I32)

    def start_of(e):
        return jnp.sum(jnp.where(e[..., None] == experts, padded_start, 0), axis=-1)

    dest = jnp.concatenate([start_of(ri[:, 0]) + ri[:, 2], start_of(ri[:, 1]) + ri[:, 3]], axis=-1)
    n_blk = cap // MOE_BLOCK
    blk_start = jnp.arange(n_blk, dtype=I32) * MOE_BLOCK
    blk_expert = jnp.minimum(jnp.sum(blk_start[:, None] >= padded_end[None, :], axis=-1), N_EXPERTS - 1)
    n_used = (padded_end[-1:] // MOE_BLOCK).astype(I32)
    pad_start = (padded_start + cnt).astype(I32)
    pad_len = (padded - cnt).astype(I32)
    return dest.astype(I32)[:, None, :], blk_expert.astype(I32), n_used, pad_start, pad_len


def _rope_tables(seq, t):
    pos = jnp.arange(seq)
    row = (pos // GRID_W).astype(F32)
    col = (pos % GRID_W).astype(F32)
    n_freq = MLA_ROPE // 4
    inv_freq = ROPE_BASE ** (-jnp.arange(n_freq, dtype=F32) / n_freq)
    ang_r = row[:, None] * inv_freq
    ang_c = col[:, None] * inv_freq
    ang = jnp.concatenate([ang_r, ang_r, ang_c, ang_c], axis=-1)
    cos = jnp.concatenate([jnp.cos(ang), jnp.ones((t, MLA_ROPE), F32)], axis=0)
    sin = jnp.concatenate([jnp.sin(ang), jnp.zeros((t, MLA_ROPE), F32)], axis=0)
    n = seq + t
    pad = jnp.zeros((n, HEAD_PAD - MLA_NOPE - MLA_ROPE), F32)
    cos_q = jnp.concatenate([jnp.ones((n, MLA_NOPE), F32), cos, pad], axis=1)
    cos_k = jnp.concatenate([jnp.zeros((n, MLA_NOPE), F32), cos, pad], axis=1)
    sin_t = jnp.concatenate([jnp.zeros((n, MLA_NOPE), F32), sin, pad], axis=1)
    return cos_q, cos_k, sin_t


_ROT_SRC = np.concatenate([np.arange(8, 16), np.arange(0, 8), np.arange(24, 32), np.arange(16, 24)])
_ROT_SIGN = np.concatenate([-np.ones(8), np.ones(8), -np.ones(8), np.ones(8)]).astype(np.float32)


def _rot_cols(w):
    return w[..., _ROT_SRC] * _ROT_SIGN


def _layer_weights(p, l):
    d = p["w_in"].shape[1]
    w_in = p["w_in"][l]
    kr = w_in[:, KR0:KR0 + MLA_ROPE]
    zl = jnp.zeros((d, MLA_NOPE), F32)
    zr = jnp.zeros((d, HEAD_PAD - MLA_NOPE - MLA_ROPE), F32)
    w_in_ext = jnp.concatenate([w_in[:, :KR0], zl, kr, zr, zl, _rot_cols(kr), zr], axis=1)

    w_uq = p["mla_w_uq"][l].reshape(MLA_Q_RANK, N_HEADS, MLA_NOPE + MLA_ROPE)
    zq = jnp.zeros((MLA_Q_RANK, N_HEADS, HEAD_PAD - MLA_NOPE - MLA_ROPE), F32)
    q_plain = jnp.concatenate([w_uq, zq], axis=-1).reshape(MLA_Q_RANK, N_HEADS * HEAD_PAD)
    q_rot = jnp.concatenate([jnp.zeros((MLA_Q_RANK, N_HEADS, MLA_NOPE), F32),
                             _rot_cols(w_uq[..., MLA_NOPE:]), zq], axis=-1).reshape(MLA_Q_RANK, N_HEADS * HEAD_PAD)
    w_q = jnp.concatenate([q_plain, q_rot], axis=1)

    w_ukv = p["mla_w_ukv"][l].reshape(MLA_KV_RANK, N_HEADS, MLA_NOPE + HEAD_DIM)
    k_cols = jnp.concatenate([w_ukv[..., :MLA_NOPE], jnp.zeros((MLA_KV_RANK, N_HEADS, HEAD_PAD - MLA_NOPE), F32)],
                             axis=-1).reshape(MLA_KV_RANK, N_HEADS * HEAD_PAD)
    v_cols = w_ukv[..., MLA_NOPE:].reshape(MLA_KV_RANK, GROUP_W)
    w_kv = jnp.concatenate([k_cols, v_cols], axis=1)

    w_r = jnp.concatenate([p["w_grp"][l], p["w_exp"][l], jnp.zeros((d, LANES - N_GROUPS - N_EXPERTS), F32)], axis=1)
    wr_hi = w_r.astype(BF16)
    b_r = jnp.concatenate([p["b_grp"][l], p["b_exp"][l], jnp.zeros((LANES - N_GROUPS - N_EXPERTS,), F32)])[None]
    g_out = p["out_norm_g"][l]
    return dict(
        n1g=p["norm1_g"][l][None], n2g=p["norm2_g"][l][None], w_in=w_in_ext.astype(BF16),
        sg_w=p["sg_w"][l].astype(BF16), sg_b=jnp.repeat(p["sg_b"][l].T, HEAD_DIM, axis=1),
        g_a=g_out[None, 0:GROUP_W], g_b=g_out[None, GROUP_W:2 * GROUP_W],
        g_c=g_out[None, 2 * GROUP_W:3 * GROUP_W], g_d=g_out[None, 3 * GROUP_W:4 * GROUP_W],
        qng=p["mla_q_norm_g"][l][None], w_q=w_q.astype(BF16), kvg=p["mla_kv_norm_g"][l][None],
        w_kv=w_kv.astype(BF16), conv_w=p["conv_w"][l], w_out=p["w_out"][l].astype(BF16),
        wr_hi=wr_hi, wr_lo=(w_r - wr_hi.astype(F32)).astype(BF16), b_r=b_r)


def kernel(x, c, ctx, c_ctx, w_ada, b_ada, norm1_g, norm2_g, w_in, conv_w, sg_w, sg_b, na_rpb, mla_q_norm_g, mla_w_uq, mla_kv_norm_g, mla_w_ukv, out_norm_g, w_out, w_grp, b_grp, w_exp, b_exp, w_gate, w_up, w_down, final_norm_g):
    n_batch, seq, d = x.shape
    ctx_len = ctx.shape[1]
    n_layers = w_ada.shape[0]
    t = 512 if (n_batch * ctx_len) % 512 == 0 else 256
    assert seq % 512 == 0 and seq % ctx_len == 0 and t % ctx_len == 0 and n_batch < 8
    p = dict(w_in=w_in, mla_w_uq=mla_w_uq, mla_w_ukv=mla_w_ukv, w_grp=w_grp, w_exp=w_exp, b_grp=b_grp,
             b_exp=b_exp, out_norm_g=out_norm_g, norm1_g=norm1_g, norm2_g=norm2_g, sg_w=sg_w, sg_b=sg_b,
             mla_q_norm_g=mla_q_norm_g, mla_kv_norm_g=mla_kv_norm_g, conv_w=conv_w, w_out=w_out)

    n_lat = n_batch * seq // t
    spt = seq // t
    nt = n_batch * (seq + ctx_len)
    geo = dict(n_batch=n_batch, seq=seq, ctx_len=ctx_len)
    tile = dict(t=t, n_lat=n_lat, spt=spt, n_batch=n_batch)

    x_all = jnp.concatenate([x.reshape(n_batch * seq, d), ctx.reshape(n_batch * ctx_len, d)], axis=0)
    cc = jnp.concatenate([c, c_ctx[None], jnp.zeros((8 - n_batch - 1, d), F32)], axis=0)
    mod_all = _modulation(cc, w_ada, b_ada).reshape(n_layers, 8, N_MOD, d)
    tabs = _rope_tables(seq, t)
    final_g = final_norm_g[None]

    for l in range(n_layers):
        update_ctx = l < n_layers - 1
        lw = _layer_weights(p, l)
        mod6 = mod_all[l]
        u, bg, yb, qn, kn, vn, qm, km, vm = _proj(x_all, mod6, lw, tabs, **tile)
        bias = _nbr_bias_tables(na_rpb[l], seq // GRID_W)
        yc = _nbr_attention(qn, kn, vn, bias, lw["g_c"], **geo)
        yd = _mla_attention(qm, km, vm, lw["g_d"], **geo)
        if update_ctx:
            yc_ctx, yd_ctx = _ctx_attention(qn, kn, vn, qm, km, vm, lw["g_c"], lw["g_d"], **geo)
        else:
            yc_ctx = yd_ctx = jnp.zeros((n_batch * ctx_len, GROUP_W), BF16)
        n_run = nt // t if update_ctx else n_lat
        x_mid, h2, ri, rf, counts = _merge(x_all, u, bg, yb, yc, yd, yc_ctx, yd_ctx, mod6, lw, n_run=n_run,
                                           seq=seq, ctx_len=ctx_len, **tile)
        n_asg = 2 * n_run * t
        cap = (-(-n_asg // MOE_BLOCK) + N_EXPERTS) * MOE_BLOCK
        dest, blk_expert, n_used, pad_start, pad_len = _route_plan(ri, counts, cap)
        x_buf = _dispatch(dest, pad_start, pad_len, n_used, h2, cap, t=t)
        y_buf = _experts(blk_expert, n_used, x_buf, w_gate, w_up, w_down, l)
        x_all = _combine(dest, x_mid, rf, mod6, final_g, y_buf, final=not update_ctx, **tile)
    return x_all.reshape(n_batch, seq, d)
```
